```python
import math
import jax, jax.numpy as jnp
from jax import lax
import numpy as np

D_MODEL = 1024
BATCH = 8
SEQ = 2048
DEPTH = 1

CHUNK = 64
SSD_HEADS = 16
SSD_HEAD_DIM = 64
SSD_INNER = SSD_HEADS * SSD_HEAD_DIM
SSD_GROUPS = 2
SSD_STATE = 128
SSD_CONV = 4
SSD_SCAN_CHUNK = CHUNK
CONV_DIM = SSD_INNER + 2 * SSD_GROUPS * SSD_STATE
SB_HEADS = 16
SB_HEAD_DIM = 64
SB_INNER = SB_HEADS * SB_HEAD_DIM
SB_BLOCK = 128
N_BRANCH = 2
IN_SIZES = (SSD_INNER, CONV_DIM, SSD_HEADS, SB_INNER, SB_INNER, SB_INNER, SB_INNER, N_BRANCH * D_MODEL)
IN_COLS = sum(IN_SIZES)
EPS = 1e-6

kernel_name = "hybrid_ssd_stickbreaking_gated_block"


def _split_points():
    pts, acc = [], 0
    for s in IN_SIZES[:-1]:
        acc += s
        pts.append(acc)
    return pts


def rmsnorm(x, w):
    xf = x.astype(jnp.float32)
    y = xf * lax.rsqrt(jnp.mean(xf * xf, axis=-1, keepdims=True) + EPS)
    return (y * w.astype(jnp.float32)).astype(x.dtype)


def gated_group_rmsnorm(y, z, w, groups):
    g = (y.astype(jnp.float32) * jax.nn.silu(z.astype(jnp.float32)))
    g = g.reshape(y.shape[:-1] + (groups, y.shape[-1] // groups))
    g = g * lax.rsqrt(jnp.mean(g * g, axis=-1, keepdims=True) + EPS)
    return (g.reshape(y.shape) * w.astype(jnp.float32)).astype(y.dtype)


def ssd_scan(xh, dt, a, bmat, cmat, d_skip):
    bsz, L, H, P = xh.shape
    G, N = bmat.shape[2], bmat.shape[3]
    R = H // G
    T = SSD_SCAN_CHUNK
    NC = L // T
    x_dt = (xh * dt[..., None]).reshape(bsz, NC, T, G, R, P)
    a_dt = jnp.transpose((dt * a).reshape(bsz, NC, T, G, R), (0, 3, 4, 1, 2))
    a_cs = jnp.cumsum(a_dt, axis=-1)
    bc = bmat.reshape(bsz, NC, T, G, N)
    cc = cmat.reshape(bsz, NC, T, G, N)
    seg = a_cs[..., :, None] - a_cs[..., None, :]
    causal = jnp.tril(jnp.ones((T, T), dtype=bool))
    decay = jnp.exp(jnp.where(causal, seg, -jnp.inf))
    scores = jnp.einsum('bclgn,bcsgn->bgcls', cc, bc)
    attn = scores[:, :, None] * decay
    y_diag = jnp.einsum('bgrcls,bcsgrp->bclgrp', attn, x_dt)
    decay_states = jnp.exp(a_cs[..., -1:] - a_cs)
    states = jnp.einsum('bclgn,bgrcl,bclgrp->bcgrpn', bc, decay_states, x_dt)
    chunk_decay = jnp.exp(a_cs[..., -1])

    def step(carry, inp):
        st, dec = inp
        new = carry * dec[..., None, None] + st
        return new, carry

    init = jnp.zeros((bsz, G, R, P, N), dtype=xh.dtype)
    _, prev = lax.scan(step, init, (jnp.moveaxis(states, 1, 0), jnp.moveaxis(chunk_decay, -1, 0)))
    prev = jnp.moveaxis(prev, 0, 1)
    y_off = jnp.einsum('bclgn,bcgrpn,bgrcl->bclgrp', cc, prev, jnp.exp(a_cs))
    y = (y_diag + y_off).reshape(bsz, L, H, P)
    return y + xh * d_skip[:, None]


def stick_breaking(q, k, v):
    bsz, L, H, Dh = q.shape
    scale = Dh ** -0.5
    outs = []
    for blk in range(L // SB_BLOCK):
        q0 = blk * SB_BLOCK
        kend = q0 + SB_BLOCK
        logits = jnp.einsum('bqhd,bkhd->bhqk', q[:, q0:kend], k[:, :kend]).astype(jnp.float32) * scale
        qi = q0 + jnp.arange(SB_BLOCK)[:, None]
        ki = jnp.arange(kend)[None, :]
        mask = ki < qi
        log_fail = jnp.where(mask, jax.nn.log_sigmoid(-logits), 0.0)
        between = lax.cumsum(log_fail, axis=3, reverse=True) - log_fail
        w = jnp.where(mask, jnp.exp(jax.nn.log_sigmoid(logits) + between), 0.0)
        outs.append(jnp.einsum('bhqk,bkhd->bqhd', w.astype(v.dtype), v[:, :kend]))
    return jnp.concatenate(outs, axis=1)


def hybrid_layer(x, norm_pre, w_in, b_gate, conv_w, conv_b, dt_bias, a_log, d_skip,
                 ssd_norm, w_ssd_proj, w_sb_proj, w_out, norm_post):
    bsz, L, _ = x.shape
    h = rmsnorm(x, norm_pre)
    proj = jnp.einsum('bld,de->ble', h, w_in)
    z_ssd, xbc, dt_raw, q, k, v, z_sb, gates = jnp.split(proj, _split_points(), axis=-1)

    xbc = lax.conv_general_dilated(
        xbc, conv_w[:, None, :], window_strides=(1,), padding=[(SSD_CONV - 1, 0)],
        dimension_numbers=('NWC', 'WIO', 'NWC'), feature_group_count=CONV_DIM) + conv_b
    xbc = jax.nn.silu(xbc)
    xs, bm, cm = jnp.split(xbc, [SSD_INNER, SSD_INNER + SSD_GROUPS * SSD_STATE], axis=-1)
    dt = jax.nn.softplus(dt_raw.astype(jnp.float32) + dt_bias.astype(jnp.float32))
    a = -jnp.exp(a_log.astype(jnp.float32))
    y = ssd_scan(xs.reshape(bsz, L, SSD_HEADS, SSD_HEAD_DIM).astype(jnp.float32), dt, a,
                 bm.reshape(bsz, L, SSD_GROUPS, SSD_STATE).astype(jnp.float32),
                 cm.reshape(bsz, L, SSD_GROUPS, SSD_STATE).astype(jnp.float32),
                 d_skip.astype(jnp.float32))
    y = y.reshape(bsz, L, SSD_INNER).astype(x.dtype)
    y = gated_group_rmsnorm(y, z_ssd, ssd_norm, SSD_GROUPS)
    y_ssd = jnp.einsum('ble,ed->bld', y, w_ssd_proj)

    o = stick_breaking(q.reshape(bsz, L, SB_HEADS, SB_HEAD_DIM),
                       k.reshape(bsz, L, SB_HEADS, SB_HEAD_DIM),
                       v.reshape(bsz, L, SB_HEADS, SB_HEAD_DIM)).reshape(bsz, L, SB_INNER)
    y_sb = jnp.einsum('ble,ed->bld', o * jax.nn.silu(z_sb), w_sb_proj)

    g = jax.nn.sigmoid(gates + b_gate)
    g_ssd, g_sb = jnp.split(g, [D_MODEL], axis=-1)
    merged = g_ssd * y_ssd + g_sb * y_sb
    out = jnp.einsum('bld,de->ble', merged, w_out)
    return x + rmsnorm(out, norm_post)


def setup_inputs(seed: int = 0) -> dict:
    key = jax.random.key(seed)
    ks = jax.random.split(key, 16)
    f32 = jnp.float32
    x = jax.random.normal(ks[0], (BATCH, SEQ, D_MODEL), f32)
    norm_pre = 1.0 + 0.05 * jax.random.normal(ks[1], (DEPTH, D_MODEL), f32)
    w_in = jax.random.normal(ks[2], (DEPTH, D_MODEL, IN_COLS), f32) * D_MODEL ** -0.5
    b_gate = 0.02 * jax.random.normal(ks[3], (DEPTH, N_BRANCH * D_MODEL), f32)
    conv_w = jax.random.normal(ks[4], (DEPTH, SSD_CONV, CONV_DIM), f32) * SSD_CONV ** -0.5
    conv_b = 0.02 * jax.random.normal(ks[5], (DEPTH, CONV_DIM), f32)
    u = jax.random.uniform(ks[6], (DEPTH, SSD_HEADS), f32)
    dt0 = jnp.exp(u * (math.log(0.1) - math.log(0.001)) + math.log(0.001))
    dt_bias = dt0 + jnp.log(-jnp.expm1(-dt0))
    a_log = jnp.log(jax.random.uniform(ks[7], (DEPTH, SSD_HEADS), f32, 1.0, 16.0))
    d_skip = 1.0 + 0.05 * jax.random.normal(ks[8], (DEPTH, SSD_HEADS), f32)
    ssd_norm = 1.0 + 0.05 * jax.random.normal(ks[9], (DEPTH, SSD_INNER), f32)
    w_ssd_proj = jax.random.normal(ks[10], (DEPTH, SSD_INNER, D_MODEL), f32) * SSD_INNER ** -0.5
    w_sb_proj = jax.random.normal(ks[11], (DEPTH, SB_INNER, D_MODEL), f32) * SB_INNER ** -0.5
    w_out = jax.random.normal(ks[12], (DEPTH, D_MODEL, D_MODEL), f32) * D_MODEL ** -0.5
    norm_post = 1.0 + 0.05 * jax.random.normal(ks[13], (DEPTH, D_MODEL), f32)
    return {"x": x, "norm_pre": norm_pre, "w_in": w_in, "b_gate": b_gate,
            "conv_w": conv_w, "conv_b": conv_b, "dt_bias": dt_bias, "a_log": a_log,
            "d_skip": d_skip, "ssd_norm": ssd_norm, "w_ssd_proj": w_ssd_proj,
            "w_sb_proj": w_sb_proj, "w_out": w_out, "norm_post": norm_post}


def reference(x, norm_pre, w_in, b_gate, conv_w, conv_b, dt_bias, a_log, d_skip,
              ssd_norm, w_ssd_proj, w_sb_proj, w_out, norm_post):
    for layer in range(DEPTH):
        x = hybrid_layer(x, norm_pre[layer], w_in[layer], b_gate[layer], conv_w[layer],
                         conv_b[layer], dt_bias[layer], a_log[layer], d_skip[layer],
                         ssd_norm[layer], w_ssd_proj[layer], w_sb_proj[layer],
                         w_out[layer], norm_post[layer])
    return x
```

```python
import functools

import numpy as np
import jax
import jax.numpy as jnp
from jax import lax
from jax.experimental import pallas as pl
from jax.experimental.pallas import tpu as pltpu

F32 = jnp.float32
BF16 = jnp.bfloat16
HIGHEST = lax.Precision.HIGHEST

D_MODEL = 1024
EPS = 1e-6
LANES = 128

SSD_HEADS = 16
SSD_HEAD_DIM = 64
SSD_INNER = SSD_HEADS * SSD_HEAD_DIM
SSD_GROUPS = 2
SSD_STATE = 128
SSD_CONV = 4
HEADS_PER_GROUP = SSD_HEADS // SSD_GROUPS
GROUP_WIDTH = SSD_INNER // SSD_GROUPS
BC_WIDTH = 2 * SSD_GROUPS * SSD_STATE
CONV_DIM = SSD_INNER + BC_WIDTH
CONV_HALO = 8

SB_HEADS = 16
SB_HEAD_DIM = 64
SB_INNER = SB_HEADS * SB_HEAD_DIM
SB_BLOCK = 128
SB_PAIR = 2

COL_ZSSD = 0
COL_Q = 1024
COL_K = 2048
COL_V = 3072
COL_ZSB = 4096
COL_GATE = 5120
COL_XS = 7168
COL_BC = 8192
PROJ_COLS = 8704

_SRC_ZSSD = 0
_SRC_XBC = _SRC_ZSSD + SSD_INNER
_SRC_DT = _SRC_XBC + CONV_DIM
_SRC_Q = _SRC_DT + SSD_HEADS
_SRC_K = _SRC_Q + SB_INNER
_SRC_V = _SRC_K + SB_INNER
_SRC_ZSB = _SRC_V + SB_INNER
_SRC_GATE = _SRC_ZSB + SB_INNER

VMEM_LIMIT = 56 * 1024 * 1024


def _softplus(x):
    return jnp.maximum(x, 0.0) + jnp.log1p(jnp.exp(-jnp.abs(x)))


def _sigmoid(x):
    return 1.0 / (1.0 + jnp.exp(-x))


def _inproj_kernel(x_ref, nw_ref, w_ref, wdt_ref, proj_ref, dt_ref, h_ref):
    @pl.when(pl.program_id(1) == 0)
    def _():
        x = x_ref[...]
        ms = jnp.mean(x * x, axis=-1, keepdims=True)
        h = (x * lax.rsqrt(ms + EPS) * nw_ref[...]).astype(BF16)
        h_ref[...] = h
        dt_ref[...] = jnp.dot(h, wdt_ref[...], preferred_element_type=F32)

    proj_ref[...] = jnp.dot(h_ref[...], w_ref[...], preferred_element_type=F32).astype(BF16)


def _inproj(x2, norm_pre, w_r, w_dt, tm, tn):
    m = x2.shape[0]
    return pl.pallas_call(
        _inproj_kernel,
        grid=(m // tm, PROJ_COLS // tn),
        in_specs=[
            pl.BlockSpec((tm, D_MODEL), lambda i, j: (i, 0)),
            pl.BlockSpec((1, D_MODEL), lambda i, j: (0, 0)),
            pl.BlockSpec((D_MODEL, tn), lambda i, j: (0, j)),
            pl.BlockSpec((D_MODEL, LANES), lambda i, j: (0, 0)),
        ],
        out_specs=[
            pl.BlockSpec((tm, tn), lambda i, j: (i, j)),
            pl.BlockSpec((tm, LANES), lambda i, j: (i, 0)),
        ],
        out_shape=[
            jax.ShapeDtypeStruct((m, PROJ_COLS), BF16),
            jax.ShapeDtypeStruct((m, LANES), F32),
        ],
        scratch_shapes=[pltpu.VMEM((tm, D_MODEL), BF16)],
        compiler_params=pltpu.CompilerParams(
            dimension_semantics=("parallel", "arbitrary"), vmem_limit_bytes=VMEM_LIMIT),
        name="inproj",
    )(x2, norm_pre, w_r, w_dt)


def _ssd_kernel(z_ref, xs_ref, bc_ref, dt_ref, cw_ref, cb_ref, dtb_ref, alog_ref, dsk_ref,
                nw_ref, yn_ref, state_ref, xpad_ref, y_ref, *, t):
    c = pl.program_id(1)

    @pl.when(c == 0)
    def _():
        state_ref[...] = jnp.zeros_like(state_ref)
        xpad_ref[pl.ds(t, CONV_HALO), :] = jnp.zeros((CONV_HALO, CONV_DIM), F32)

    xpad_ref[pl.ds(0, CONV_HALO), :] = xpad_ref[pl.ds(t, CONV_HALO), :]
    xpad_ref[pl.ds(CONV_HALO, t), pl.ds(0, SSD_INNER)] = xs_ref[...].astype(F32)
    xpad_ref[pl.ds(CONV_HALO, t), pl.ds(SSD_INNER, BC_WIDTH)] = bc_ref[...].astype(F32)
    conv = jnp.broadcast_to(cb_ref[...], (t, CONV_DIM))
    for j in range(SSD_CONV):
        conv = conv + cw_ref[pl.ds(j, 1), :] * xpad_ref[pl.ds(CONV_HALO - (SSD_CONV - 1) + j, t), :]
    xbc = conv * _sigmoid(conv)
    xs = xbc[:, :SSD_INNER]

    dt = _softplus(dt_ref[...] + dtb_ref[...])
    a_dt = dt * (-jnp.exp(alog_ref[...]))
    ri = lax.broadcasted_iota(jnp.int32, (t, t), 0)
    ci = lax.broadcasted_iota(jnp.int32, (t, t), 1)
    causal = ci <= ri
    a_cs = jnp.dot(causal.astype(F32), a_dt, preferred_element_type=F32, precision=HIGHEST)
    a_cs_t = a_cs.T

    ek = lax.broadcasted_iota(jnp.int32, (LANES, SSD_INNER), 0)
    ec = lax.broadcasted_iota(jnp.int32, (LANES, SSD_INNER), 1)
    expand = (ec // SSD_HEAD_DIM == ek).astype(F32)
    dt_e = jnp.dot(dt, expand, preferred_element_type=F32, precision=HIGHEST)
    acs_e = jnp.dot(a_cs, expand, preferred_element_type=F32, precision=HIGHEST)
    alast_e = acs_e[t - 1:t, :]
    x_dt = xs * dt_e
    x_dt_b = x_dt.astype(BF16)
    x_ds_b = (x_dt * jnp.exp(alast_e - acs_e)).astype(BF16)
    ecs_e = jnp.exp(acs_e)
    chunk_decay_e = jnp.exp(alast_e)

    lane = lax.broadcasted_iota(jnp.int32, (t, LANES), 1)
    first_head = lane < SSD_HEAD_DIM
    for g in range(SSD_GROUPS):
        b_g = xbc[:, SSD_INNER + g * SSD_STATE:SSD_INNER + (g + 1) * SSD_STATE].astype(BF16)
        c_g = xbc[:, SSD_INNER + (SSD_GROUPS + g) * SSD_STATE:
                  SSD_INNER + (SSD_GROUPS + g + 1) * SSD_STATE].astype(BF16)
        scores = lax.dot_general(c_g, b_g, (((1,), (1,)), ((), ())), preferred_element_type=F32)
        for pr in range(HEADS_PER_GROUP // 2):
            col0 = g * GROUP_WIDTH + pr * LANES
            x_pair = x_dt_b[:, col0:col0 + LANES]
            halves = []
            for hh in range(2):
                h = g * HEADS_PER_GROUP + pr * 2 + hh
                seg = a_cs[:, h:h + 1] - a_cs_t[h:h + 1, :]
                decay = jnp.exp(jnp.where(causal, seg, -jnp.inf))
                attn = (scores * decay).astype(BF16)
                halves.append(jnp.dot(attn, x_pair, preferred_element_type=F32))
            y_ref[:, pl.ds(col0, LANES)] = jnp.where(first_head, halves[0], halves[1])

        gs = pl.ds(g * GROUP_WIDTH, GROUP_WIDTH)
        st = state_ref[g]
        y_off = jnp.dot(c_g, st.astype(BF16), preferred_element_type=F32)
        y_ref[:, gs] = y_ref[:, gs] + y_off * ecs_e[:, g * GROUP_WIDTH:(g + 1) * GROUP_WIDTH]
        upd = lax.dot_general(b_g, x_ds_b[:, g * GROUP_WIDTH:(g + 1) * GROUP_WIDTH],
                              (((0,), (0,)), ((), ())), preferred_element_type=F32)
        state_ref[g] = st * chunk_decay_e[:, g * GROUP_WIDTH:(g + 1) * GROUP_WIDTH] + upd

    y = y_ref[...] + xs * dsk_ref[...]
    z = z_ref[...].astype(F32)
    gated = y * (z * _sigmoid(z))
    for g in range(SSD_GROUPS):
        gg = gated[:, g * GROUP_WIDTH:(g + 1) * GROUP_WIDTH]
        ms = jnp.mean(gg * gg, axis=-1, keepdims=True)
        yn_ref[:, pl.ds(g * GROUP_WIDTH, GROUP_WIDTH)] = (
            gg * lax.rsqrt(ms + EPS) * nw_ref[:, pl.ds(g * GROUP_WIDTH, GROUP_WIDTH)]).astype(BF16)


def _ssd(proj, dt_raw, conv_w, conv_b, dt_bias, a_log, d_skip_e, ssd_norm, bsz, seq, t):
    m = bsz * seq
    nc = seq // t
    row = lambda b, c: b * nc + c
    const = lambda b, c: (0, 0)
    return pl.pallas_call(
        functools.partial(_ssd_kernel, t=t),
        grid=(bsz, nc),
        in_specs=[
            pl.BlockSpec((t, SSD_INNER), lambda b, c: (row(b, c), COL_ZSSD // SSD_INNER)),
            pl.BlockSpec((t, SSD_INNER), lambda b, c: (row(b, c), COL_XS // SSD_INNER)),
            pl.BlockSpec((t, BC_WIDTH), lambda b, c: (row(b, c), COL_BC // BC_WIDTH)),
            pl.BlockSpec((t, LANES), lambda b, c: (row(b, c), 0)),
            pl.BlockSpec((SSD_CONV, CONV_DIM), const),
            pl.BlockSpec((1, CONV_DIM), const),
            pl.BlockSpec((1, LANES), const),
            pl.BlockSpec((1, LANES), const),
            pl.BlockSpec((1, SSD_INNER), const),
            pl.BlockSpec((1, SSD_INNER), const),
        ],
        out_specs=pl.BlockSpec((t, SSD_INNER), lambda b, c: (row(b, c), 0)),
        out_shape=jax.ShapeDtypeStruct((m, SSD_INNER), BF16),
        scratch_shapes=[
            pltpu.VMEM((SSD_GROUPS, SSD_STATE, GROUP_WIDTH), F32),
            pltpu.VMEM((t + CONV_HALO, CONV_DIM), F32),
            pltpu.VMEM((t, SSD_INNER), F32),
        ],
        compiler_params=pltpu.CompilerParams(
            dimension_semantics=("parallel", "arbitrary"), vmem_limit_bytes=VMEM_LIMIT),
        name="ssd",
    )(proj, proj, proj, dt_raw, conv_w, conv_b, dt_bias, a_log, d_skip_e, ssd_norm)


def _suffix_sum_matrix():
    j = np.arange(2 * SB_BLOCK)[:, None] % SB_BLOCK
    n = np.arange(2 * SB_BLOCK)[None, :]
    return np.where(n < SB_BLOCK, j > n, True).astype(np.float32)


def _sb_kernel(q_ref, k_ref, v_ref, zsb_ref, uu_ref, o_ref):
    i = pl.program_id(2)
    blk = SB_BLOCK
    lane = lax.broadcasted_iota(jnp.int32, (blk, LANES), 1)
    first_head = lane < SB_HEAD_DIM
    q = q_ref[...] * jnp.asarray(SB_HEAD_DIM ** -0.5, BF16)
    zero = jnp.zeros_like(q)
    qm = (jnp.where(first_head, q, zero), jnp.where(first_head, zero, q))
    ri = lax.broadcasted_iota(jnp.int32, (blk, blk), 0)
    ci = lax.broadcasted_iota(jnp.int32, (blk, blk), 1)
    strict = ci < ri
    uu = uu_ref[...]

    def block(j, acc, car, diagonal):
        k_j = k_ref[pl.ds(pl.multiple_of(j * blk, blk), blk), :]
        v_j = v_ref[pl.ds(pl.multiple_of(j * blk, blk), blk), :]
        logits, sps, cats = [], [], []
        for h in range(SB_PAIR):
            s = lax.dot_general(qm[h], k_j, (((1,), (1,)), ((), ())), preferred_element_type=F32)
            sp = _softplus(s)
            spm = jnp.where(strict, sp, 0.0) if diagonal else sp
            hi = spm.astype(BF16)
            lo = (spm - hi.astype(F32)).astype(BF16)
            logits.append(s)
            sps.append(sp)
            cats.append(jnp.concatenate([hi, lo], axis=1))
        cs = jnp.dot(jnp.concatenate(cats, axis=0), uu, preferred_element_type=F32)
        new_acc, new_car = [], []
        for h in range(SB_PAIR):
            suffix = cs[h * blk:(h + 1) * blk, :blk]
            total = cs[h * blk:(h + 1) * blk, blk:]
            w = jnp.exp(logits[h] - sps[h] - suffix - car[h])
            if diagonal:
                w = jnp.where(strict, w, 0.0)
            new_acc.append(acc[h] + jnp.dot(w.astype(BF16), v_j, preferred_element_type=F32))
            new_car.append(car[h] + total)
        return tuple(new_acc), tuple(new_car)

    zeros = tuple(jnp.zeros((blk, LANES), F32) for _ in range(SB_PAIR))
    acc, car = block(i, zeros, zeros, True)

    def body(jj, carry):
        return block(i - 1 - jj, carry[0], carry[1], False)

    acc, car = lax.fori_loop(0, i, body, (acc, car))
    o = jnp.where(first_head, acc[0], acc[1])
    z = zsb_ref[...].astype(F32)
    o_ref[...] = (o * (z * _sigmoid(z))).astype(BF16)


def _stick_breaking(proj, uu, bsz, seq):
    m = bsz * seq
    nq = seq // SB_BLOCK
    npair = SB_HEADS // SB_PAIR
    return pl.pallas_call(
        _sb_kernel,
        grid=(bsz, npair, nq),
        in_specs=[
            pl.BlockSpec((SB_BLOCK, LANES), lambda b, p, i: (b * nq + i, COL_Q // LANES + p)),
            pl.BlockSpec((seq, LANES), lambda b, p, i: (b, COL_K // LANES + p)),
            pl.BlockSpec((seq, LANES), lambda b, p, i: (b, COL_V // LANES + p)),
            pl.BlockSpec((SB_BLOCK, LANES), lambda b, p, i: (b * nq + i, COL_ZSB // LANES + p)),
            pl.BlockSpec((2 * SB_BLOCK, 2 * SB_BLOCK), lambda b, p, i: (0, 0)),
        ],
        out_specs=pl.BlockSpec((SB_BLOCK, LANES), lambda b, p, i: (b * nq + i, p)),
        out_shape=jax.ShapeDtypeStruct((m, SB_INNER), BF16),
        compiler_params=pltpu.CompilerParams(
            dimension_semantics=("parallel", "parallel", "arbitrary"),
            vmem_limit_bytes=VMEM_LIMIT),
        name="stick_breaking",
    )(proj, proj, proj, proj, uu)


def _merge_kernel(yn_ref, osb_ref, g1_ref, g2_ref, x_ref, bg_ref, wssd_ref, wsb_ref, wout_ref,
                  nw_ref, out_ref):
    y_ssd = jnp.dot(yn_ref[...], wssd_ref[...], preferred_element_type=F32)
    y_sb = jnp.dot(osb_ref[...], wsb_ref[...], preferred_element_type=F32)
    g_ssd = _sigmoid(g1_ref[...].astype(F32) + bg_ref[:, pl.ds(0, D_MODEL)])
    g_sb = _sigmoid(g2_ref[...].astype(F32) + bg_ref[:, pl.ds(D_MODEL, D_MODEL)])
    merged = (g_ssd * y_ssd + g_sb * y_sb).astype(BF16)
    out = jnp.dot(merged, wout_ref[...], preferred_element_type=F32)
    ms = jnp.mean(out * out, axis=-1, keepdims=True)
    out_ref[...] = x_ref[...] + out * lax.rsqrt(ms + EPS) * nw_ref[...]


def _merge(yn, osb, proj, x2, b_gate, w_ssd, w_sb, w_out, norm_post, tm):
    m = x2.shape[0]
    rows = lambda i: (i, 0)
    const = lambda i: (0, 0)
    return pl.pallas_call(
        _merge_kernel,
        grid=(m // tm,),
        in_specs=[
            pl.BlockSpec((tm, D_MODEL), rows),
            pl.BlockSpec((tm, D_MODEL), rows),
            pl.BlockSpec((tm, D_MODEL), lambda i: (i, COL_GATE // D_MODEL)),
            pl.BlockSpec((tm, D_MODEL), lambda i: (i, COL_GATE // D_MODEL + 1)),
            pl.BlockSpec((tm, D_MODEL), rows),
            pl.BlockSpec((1, 2 * D_MODEL), const),
            pl.BlockSpec((D_MODEL, D_MODEL), const),
            pl.BlockSpec((D_MODEL, D_MODEL), const),
            pl.BlockSpec((D_MODEL, D_MODEL), const),
            pl.BlockSpec((1, D_MODEL), const),
        ],
        out_specs=pl.BlockSpec((tm, D_MODEL), rows),
        out_shape=jax.ShapeDtypeStruct((m, D_MODEL), F32),
        compiler_params=pltpu.CompilerParams(
            dimension_semantics=("parallel",), vmem_limit_bytes=VMEM_LIMIT),
        name="merge",
    )(yn, osb, proj, proj, x2, b_gate, w_ssd, w_sb, w_out, norm_post)


def _layer(x, norm_pre, w_in, b_gate, conv_w, conv_b, dt_bias, a_log, d_skip, ssd_norm,
           w_ssd_proj, w_sb_proj, w_out, norm_post):
    bsz, seq, _ = x.shape
    m = bsz * seq
    x2 = x.reshape(m, D_MODEL)

    seg = lambda start, width: w_in[:, start:start + width]
    w_r = jnp.concatenate([
        seg(_SRC_ZSSD, SSD_INNER), seg(_SRC_Q, SB_INNER), seg(_SRC_K, SB_INNER),
        seg(_SRC_V, SB_INNER), seg(_SRC_ZSB, SB_INNER), seg(_SRC_GATE, 2 * D_MODEL),
        seg(_SRC_XBC, CONV_DIM)], axis=1).astype(BF16)
    w_dt = jnp.pad(seg(_SRC_DT, SSD_HEADS), ((0, 0), (0, LANES - SSD_HEADS))).astype(BF16)
    pad_heads = lambda v: jnp.pad(v, (0, LANES - SSD_HEADS)).reshape(1, LANES)

    tm = min(512, m)
    proj, dt_raw = _inproj(x2, norm_pre.reshape(1, D_MODEL), w_r, w_dt, tm, 512)

    t = min(128, seq)
    yn = _ssd(proj, dt_raw, conv_w, conv_b.reshape(1, CONV_DIM), pad_heads(dt_bias),
              pad_heads(a_log), jnp.repeat(d_skip, SSD_HEAD_DIM).reshape(1, SSD_INNER),
              ssd_norm.reshape(1, SSD_INNER), bsz, seq, t)

    osb = _stick_breaking(proj, jnp.asarray(_suffix_sum_matrix(), BF16), bsz, seq)

    out = _merge(yn, osb, proj, x2, b_gate.reshape(1, 2 * D_MODEL), w_ssd_proj.astype(BF16),
                 w_sb_proj.astype(BF16), w_out.astype(BF16), norm_post.reshape(1, D_MODEL), tm)
    return out.reshape(bsz, seq, D_MODEL)


def kernel(x, norm_pre, w_in, b_gate, conv_w, conv_b, dt_bias, a_log, d_skip, ssd_norm,
           w_ssd_proj, w_sb_proj, w_out, norm_post):
    for layer in range(norm_pre.shape[0]):
        x = _layer(x, norm_pre[layer], w_in[layer], b_gate[layer], conv_w[layer],
                   conv_b[layer], dt_bias[layer], a_log[layer], d_skip[layer],
                   ssd_norm[layer], w_ssd_proj[layer], w_sb_proj[layer], w_out[layer],
                   norm_post[layer])
    return x
```

```python
import functools

import numpy as np
import jax
import jax.numpy as jnp
from jax import lax
from jax.experimental import pallas as pl
from jax.experimental.pallas import tpu as pltpu

F32 = jnp.float32
BF16 = jnp.bfloat16
HIGHEST = lax.Precision.HIGHEST

D_MODEL = 1024
EPS = 1e-6
LANES = 128

SSD_HEADS = 16
SSD_HEAD_DIM = 64
SSD_INNER = SSD_HEADS * SSD_HEAD_DIM
SSD_GROUPS = 2
SSD_STATE = 128
SSD_CONV = 4
HEADS_PER_GROUP = SSD_HEADS // SSD_GROUPS
GROUP_WIDTH = SSD_INNER // SSD_GROUPS
BC_WIDTH = 2 * SSD_GROUPS * SSD_STATE
CONV_DIM = SSD_INNER + BC_WIDTH
CONV_HALO = 8

SB_HEADS = 16
SB_HEAD_DIM = 64
SB_INNER = SB_HEADS * SB_HEAD_DIM
SB_BLOCK = 128
SB_PAIR = 2

COL_ZSSD = 0
COL_Q = 1024
COL_K = 2048
COL_V = 3072
COL_ZSB = 4096
COL_GATE = 5120
COL_XS = 7168
COL_BC = 8192
PROJ_COLS = 8704

_SRC_ZSSD = 0
_SRC_XBC = _SRC_ZSSD + SSD_INNER
_SRC_DT = _SRC_XBC + CONV_DIM
_SRC_Q = _SRC_DT + SSD_HEADS
_SRC_K = _SRC_Q + SB_INNER
_SRC_V = _SRC_K + SB_INNER
_SRC_ZSB = _SRC_V + SB_INNER
_SRC_GATE = _SRC_ZSB + SB_INNER

VMEM_LIMIT = 56 * 1024 * 1024


def _softplus(x):
    return jnp.maximum(x, 0.0) + jnp.log1p(jnp.exp(-jnp.abs(x)))


def _sigmoid(x):
    return 1.0 / (1.0 + jnp.exp(-x))


def _inproj_kernel(x_ref, nw_ref, w_ref, wdt_ref, proj_ref, dt_ref, h_ref):
    @pl.when(pl.program_id(1) == 0)
    def _():
        x = x_ref[...]
        ms = jnp.mean(x * x, axis=-1, keepdims=True)
        h = (x * lax.rsqrt(ms + EPS) * nw_ref[...]).astype(BF16)
        h_ref[...] = h
        dt_ref[...] = jnp.dot(h, wdt_ref[...], preferred_element_type=F32)

    proj_ref[...] = jnp.dot(h_ref[...], w_ref[...], preferred_element_type=F32).astype(BF16)


def _inproj(x2, norm_pre, w_r, w_dt, tm, tn):
    m = x2.shape[0]
    return pl.pallas_call(
        _inproj_kernel,
        grid=(m // tm, PROJ_COLS // tn),
        in_specs=[
            pl.BlockSpec((tm, D_MODEL), lambda i, j: (i, 0)),
            pl.BlockSpec((1, D_MODEL), lambda i, j: (0, 0)),
            pl.BlockSpec((D_MODEL, tn), lambda i, j: (0, j)),
            pl.BlockSpec((D_MODEL, LANES), lambda i, j: (0, 0)),
        ],
        out_specs=[
            pl.BlockSpec((tm, tn), lambda i, j: (i, j)),
            pl.BlockSpec((tm, LANES), lambda i, j: (i, 0)),
        ],
        out_shape=[
            jax.ShapeDtypeStruct((m, PROJ_COLS), BF16),
            jax.ShapeDtypeStruct((m, LANES), F32),
        ],
        scratch_shapes=[pltpu.VMEM((tm, D_MODEL), BF16)],
        compiler_params=pltpu.CompilerParams(
            dimension_semantics=("parallel", "arbitrary"), vmem_limit_bytes=VMEM_LIMIT),
        name="inproj",
    )(x2, norm_pre, w_r, w_dt)


def _ssd_kernel(z_ref, xs_ref, bc_ref, dt_ref, cw_ref, cb_ref, dtb_ref, alog_ref, dsk_ref,
                nw_ref, yn_ref, state_ref, xpad_ref, y_ref, *, t):
    c = pl.program_id(1)

    @pl.when(c == 0)
    def _():
        state_ref[...] = jnp.zeros_like(state_ref)
        xpad_ref[pl.ds(t, CONV_HALO), :] = jnp.zeros((CONV_HALO, CONV_DIM), F32)

    xpad_ref[pl.ds(0, CONV_HALO), :] = xpad_ref[pl.ds(t, CONV_HALO), :]
    xpad_ref[pl.ds(CONV_HALO, t), pl.ds(0, SSD_INNER)] = xs_ref[...].astype(F32)
    xpad_ref[pl.ds(CONV_HALO, t), pl.ds(SSD_INNER, BC_WIDTH)] = bc_ref[...].astype(F32)
    conv = jnp.broadcast_to(cb_ref[...], (t, CONV_DIM))
    for j in range(SSD_CONV):
        conv = conv + cw_ref[pl.ds(j, 1), :] * xpad_ref[pl.ds(CONV_HALO - (SSD_CONV - 1) + j, t), :]
    xbc = conv * _sigmoid(conv)
    xs = xbc[:, :SSD_INNER]

    dt = _softplus(dt_ref[...] + dtb_ref[...])
    a_dt = dt * (-jnp.exp(alog_ref[...]))
    ri = lax.broadcasted_iota(jnp.int32, (t, t), 0)
    ci = lax.broadcasted_iota(jnp.int32, (t, t), 1)
    causal = ci <= ri
    a_cs = jnp.dot(causal.astype(F32), a_dt, preferred_element_type=F32, precision=HIGHEST)
    a_cs_t = a_cs.T

    ek = lax.broadcasted_iota(jnp.int32, (LANES, SSD_INNER), 0)
    ec = lax.broadcasted_iota(jnp.int32, (LANES, SSD_INNER), 1)
    expand = (ec // SSD_HEAD_DIM == ek).astype(F32)
    dt_e = jnp.dot(dt, expand, preferred_element_type=F32, precision=HIGHEST)
    acs_e = jnp.dot(a_cs, expand, preferred_element_type=F32, precision=HIGHEST)
    alast_e = acs_e[t - 1:t, :]
    x_dt = xs * dt_e
    x_dt_b = x_dt.astype(BF16)
    x_ds_b = (x_dt * jnp.exp(alast_e - acs_e)).astype(BF16)
    ecs_e = jnp.exp(acs_e)
    chunk_decay_e = jnp.exp(alast_e)

    lane = lax.broadcasted_iota(jnp.int32, (t, LANES), 1)
    first_head = lane < SSD_HEAD_DIM
    for g in range(SSD_GROUPS):
        b_g = xbc[:, SSD_INNER + g * SSD_STATE:SSD_INNER + (g + 1) * SSD_STATE].astype(BF16)
        c_g = xbc[:, SSD_INNER + (SSD_GROUPS + g) * SSD_STATE:
                  SSD_INNER + (SSD_GROUPS + g + 1) * SSD_STATE].astype(BF16)
        scores = lax.dot_general(c_g, b_g, (((1,), (1,)), ((), ())), preferred_element_type=F32)
        for pr in range(HEADS_PER_GROUP // 2):
            col0 = g * GROUP_WIDTH + pr * LANES
            x_pair = x_dt_b[:, col0:col0 + LANES]
            halves = []
            for hh in range(2):
                h = g * HEADS_PER_GROUP + pr * 2 + hh
                seg = a_cs[:, h:h + 1] - a_cs_t[h:h + 1, :]
                decay = jnp.exp(jnp.where(causal, seg, -jnp.inf))
                attn = (scores * decay).astype(BF16)
                halves.append(jnp.dot(attn, x_pair, preferred_element_type=F32))
            y_ref[:, pl.ds(col0, LANES)] = jnp.where(first_head, halves[0], halves[1])

        gs = pl.ds(g * GROUP_WIDTH, GROUP_WIDTH)
        st = state_ref[g]
        y_off = jnp.dot(c_g, st.astype(BF16), preferred_element_type=F32)
        y_ref[:, gs] = y_ref[:, gs] + y_off * ecs_e[:, g * GROUP_WIDTH:(g + 1) * GROUP_WIDTH]
        upd = lax.dot_general(b_g, x_ds_b[:, g * GROUP_WIDTH:(g + 1) * GROUP_WIDTH],
                              (((0,), (0,)), ((), ())), preferred_element_type=F32)
        state_ref[g] = st * chunk_decay_e[:, g * GROUP_WIDTH:(g + 1) * GROUP_WIDTH] + upd

    y = y_ref[...] + xs * dsk_ref[...]
    z = z_ref[...].astype(F32)
    gated = y * (z * _sigmoid(z))
    for g in range(SSD_GROUPS):
        gg = gated[:, g * GROUP_WIDTH:(g + 1) * GROUP_WIDTH]
        ms = jnp.mean(gg * gg, axis=-1, keepdims=True)
        yn_ref[:, pl.ds(g * GROUP_WIDTH, GROUP_WIDTH)] = (
            gg * lax.rsqrt(ms + EPS) * nw_ref[:, pl.ds(g * GROUP_WIDTH, GROUP_WIDTH)]).astype(BF16)


def _ssd(proj, dt_raw, conv_w, conv_b, dt_bias, a_log, d_skip_e, ssd_norm, bsz, seq, t):
    m = bsz * seq
    nc = seq // t
    row = lambda b, c: b * nc + c
    const = lambda b, c: (0, 0)
    return pl.pallas_call(
        functools.partial(_ssd_kernel, t=t),
        grid=(bsz, nc),
        in_specs=[
            pl.BlockSpec((t, SSD_INNER), lambda b, c: (row(b, c), COL_ZSSD // SSD_INNER)),
            pl.BlockSpec((t, SSD_INNER), lambda b, c: (row(b, c), COL_XS // SSD_INNER)),
            pl.BlockSpec((t, BC_WIDTH), lambda b, c: (row(b, c), COL_BC // BC_WIDTH)),
            pl.BlockSpec((t, LANES), lambda b, c: (row(b, c), 0)),
            pl.BlockSpec((SSD_CONV, CONV_DIM), const),
            pl.BlockSpec((1, CONV_DIM), const),
            pl.BlockSpec((1, LANES), const),
            pl.BlockSpec((1, LANES), const),
            pl.BlockSpec((1, SSD_INNER), const),
            pl.BlockSpec((1, SSD_INNER), const),
        ],
        out_specs=pl.BlockSpec((t, SSD_INNER), lambda b, c: (row(b, c), 0)),
        out_shape=jax.ShapeDtypeStruct((m, SSD_INNER), BF16),
        scratch_shapes=[
            pltpu.VMEM((SSD_GROUPS, SSD_STATE, GROUP_WIDTH), F32),
            pltpu.VMEM((t + CONV_HALO, CONV_DIM), F32),
            pltpu.VMEM((t, SSD_INNER), F32),
        ],
        compiler_params=pltpu.CompilerParams(
            dimension_semantics=("parallel", "arbitrary"), vmem_limit_bytes=VMEM_LIMIT),
        name="ssd",
    )(proj, proj, proj, dt_raw, conv_w, conv_b, dt_bias, a_log, d_skip_e, ssd_norm)


def _suffix_sum_matrix():
    j = np.arange(2 * SB_BLOCK)[:, None] % SB_BLOCK
    n = np.arange(2 * SB_BLOCK)[None, :]
    return np.where(n < SB_BLOCK, j > n, True).astype(np.float32)


def _sb_kernel(q_ref, k_ref, v_ref, zsb_ref, uu_ref, o_ref, qm_ref, acc_ref, car_ref, *, qt):
    tile = pl.program_id(2)
    blk = SB_BLOCK
    span = 2 * blk
    first_head = lax.broadcasted_iota(jnp.int32, (qt, LANES), 1) < SB_HEAD_DIM
    q = q_ref[...] * jnp.asarray(SB_HEAD_DIM ** -0.5, BF16)
    zero = jnp.zeros_like(q)
    qm_ref[0] = jnp.where(first_head, q, zero)
    qm_ref[1] = jnp.where(first_head, zero, q)
    acc_ref[...] = jnp.zeros_like(acc_ref)
    car_ref[...] = jnp.zeros_like(car_ref)
    uu = uu_ref[...]
    v_first = lax.broadcasted_iota(jnp.int32, (span, LANES), 1) < SB_HEAD_DIM

    def two_blocks(j0, r0, nrows, diagonal):
        keys = pl.ds(pl.multiple_of(j0 * blk, blk), span)
        k2 = k_ref[keys, :]
        v2 = v_ref[keys, :]
        vzero = jnp.zeros_like(v2)
        vm = (jnp.where(v_first, v2, vzero), jnp.where(v_first, vzero, v2))
        rows = pl.ds(r0, nrows)
        if diagonal:
            qi = tile * qt + r0 + lax.broadcasted_iota(jnp.int32, (nrows, span), 0)
            ki = j0 * blk + lax.broadcasted_iota(jnp.int32, (nrows, span), 1)
            keep = ki < qi
        pv = None
        for h in range(SB_PAIR):
            s = lax.dot_general(qm_ref[h, rows, :], k2, (((1,), (1,)), ((), ())),
                                preferred_element_type=F32)
            sp = jnp.maximum(s, 0.0) + jnp.log(1.0 + jnp.exp(-jnp.abs(s)))
            spm = jnp.where(keep, sp, 0.0) if diagonal else sp
            car = car_ref[h, rows, :]
            zs = [None, None]
            for half in (1, 0):
                cols = slice(half * blk, (half + 1) * blk)
                x = spm[:, cols]
                hi = x.astype(BF16)
                lo = (x - hi.astype(F32)).astype(BF16)
                cs = jnp.dot(jnp.concatenate([hi, lo], axis=1), uu, preferred_element_type=F32)
                zs[half] = s[:, cols] - sp[:, cols] - cs[:, :blk] - car
                car = car + cs[:, blk:]
            car_ref[h, rows, :] = car
            w = jnp.exp(jnp.concatenate(zs, axis=1))
            if diagonal:
                w = jnp.where(keep, w, 0.0)
            contrib = jnp.dot(w.astype(BF16), vm[h], preferred_element_type=F32)
            pv = contrib if pv is None else pv + contrib
        acc_ref[rows, :] = acc_ref[rows, :] + pv

    for c in reversed(range(0, qt // blk, 2)):
        two_blocks(tile * (qt // blk) + c, c * blk, qt - c * blk, True)

    def body(jj, carry):
        two_blocks(tile * (qt // blk) - 2 - 2 * jj, 0, qt, False)
        return carry

    lax.fori_loop(0, tile * (qt // span), body, 0)
    z = zsb_ref[...].astype(F32)
    o_ref[...] = (acc_ref[...] * (z * _sigmoid(z))).astype(BF16)


def _stick_breaking(proj, uu, bsz, seq, qt):
    m = bsz * seq
    nq = seq // qt
    npair = SB_HEADS // SB_PAIR
    return pl.pallas_call(
        functools.partial(_sb_kernel, qt=qt),
        grid=(bsz, npair, nq),
        in_specs=[
            pl.BlockSpec((qt, LANES), lambda b, p, i: (b * nq + i, COL_Q // LANES + p)),
            pl.BlockSpec((seq, LANES), lambda b, p, i: (b, COL_K // LANES + p)),
            pl.BlockSpec((seq, LANES), lambda b, p, i: (b, COL_V // LANES + p)),
            pl.BlockSpec((qt, LANES), lambda b, p, i: (b * nq + i, COL_ZSB // LANES + p)),
            pl.BlockSpec((2 * SB_BLOCK, 2 * SB_BLOCK), lambda b, p, i: (0, 0)),
        ],
        out_specs=pl.BlockSpec((qt, LANES), lambda b, p, i: (b * nq + i, p)),
        out_shape=jax.ShapeDtypeStruct((m, SB_INNER), BF16),
        scratch_shapes=[
            pltpu.VMEM((SB_PAIR, qt, LANES), BF16),
            pltpu.VMEM((qt, LANES), F32),
            pltpu.VMEM((SB_PAIR, qt, LANES), F32),
        ],
        compiler_params=pltpu.CompilerParams(
            dimension_semantics=("parallel", "parallel", "arbitrary"),
            vmem_limit_bytes=VMEM_LIMIT),
        name="stick_breaking",
    )(proj, proj, proj, proj, uu)


def _merge_kernel(yn_ref, osb_ref, g1_ref, g2_ref, x_ref, bg_ref, wssd_ref, wsb_ref, wout_ref,
                  nw_ref, out_ref):
    y_ssd = jnp.dot(yn_ref[...], wssd_ref[...], preferred_element_type=F32)
    y_sb = jnp.dot(osb_ref[...], wsb_ref[...], preferred_element_type=F32)
    g_ssd = _sigmoid(g1_ref[...].astype(F32) + bg_ref[:, pl.ds(0, D_MODEL)])
    g_sb = _sigmoid(g2_ref[...].astype(F32) + bg_ref[:, pl.ds(D_MODEL, D_MODEL)])
    merged = (g_ssd * y_ssd + g_sb * y_sb).astype(BF16)
    out = jnp.dot(merged, wout_ref[...], preferred_element_type=F32)
    ms = jnp.mean(out * out, axis=-1, keepdims=True)
    out_ref[...] = x_ref[...] + out * lax.rsqrt(ms + EPS) * nw_ref[...]


def _merge(yn, osb, proj, x2, b_gate, w_ssd, w_sb, w_out, norm_post, tm):
    m = x2.shape[0]
    rows = lambda i: (i, 0)
    const = lambda i: (0, 0)
    return pl.pallas_call(
        _merge_kernel,
        grid=(m // tm,),
        in_specs=[
            pl.BlockSpec((tm, D_MODEL), rows),
            pl.BlockSpec((tm, D_MODEL), rows),
            pl.BlockSpec((tm, D_MODEL), lambda i: (i, COL_GATE // D_MODEL)),
            pl.BlockSpec((tm, D_MODEL), lambda i: (i, COL_GATE // D_MODEL + 1)),
            pl.BlockSpec((tm, D_MODEL), rows),
            pl.BlockSpec((1, 2 * D_MODEL), const),
            pl.BlockSpec((D_MODEL, D_MODEL), const),
            pl.BlockSpec((D_MODEL, D_MODEL), const),
            pl.BlockSpec((D_MODEL, D_MODEL), const),
            pl.BlockSpec((1, D_MODEL), const),
        ],
        out_specs=pl.BlockSpec((tm, D_MODEL), rows),
        out_shape=jax.ShapeDtypeStruct((m, D_MODEL), F32),
        compiler_params=pltpu.CompilerParams(
            dimension_semantics=("parallel",), vmem_limit_bytes=VMEM_LIMIT),
        name="merge",
    )(yn, osb, proj, proj, x2, b_gate, w_ssd, w_sb, w_out, norm_post)


def _layer(x, norm_pre, w_in, b_gate, conv_w, conv_b, dt_bias, a_log, d_skip, ssd_norm,
           w_ssd_proj, w_sb_proj, w_out, norm_post):
    bsz, seq, _ = x.shape
    m = bsz * seq
    x2 = x.reshape(m, D_MODEL)

    seg = lambda start, width: w_in[:, start:start + width]
    w_r = jnp.concatenate([
        seg(_SRC_ZSSD, SSD_INNER), seg(_SRC_Q, SB_INNER), seg(_SRC_K, SB_INNER),
        seg(_SRC_V, SB_INNER), seg(_SRC_ZSB, SB_INNER), seg(_SRC_GATE, 2 * D_MODEL),
        seg(_SRC_XBC, CONV_DIM)], axis=1).astype(BF16)
    w_dt = jnp.pad(seg(_SRC_DT, SSD_HEADS), ((0, 0), (0, LANES - SSD_HEADS))).astype(BF16)
    pad_heads = lambda v: jnp.pad(v, (0, LANES - SSD_HEADS)).reshape(1, LANES)

    tm = min(512, m)
    proj, dt_raw = _inproj(x2, norm_pre.reshape(1, D_MODEL), w_r, w_dt, tm, 512)

    t = min(128, seq)
    yn = _ssd(proj, dt_raw, conv_w, conv_b.reshape(1, CONV_DIM), pad_heads(dt_bias),
              pad_heads(a_log), jnp.repeat(d_skip, SSD_HEAD_DIM).reshape(1, SSD_INNER),
              ssd_norm.reshape(1, SSD_INNER), bsz, seq, t)

    osb = _stick_breaking(proj, jnp.asarray(_suffix_sum_matrix(), BF16), bsz, seq,
                          min(512, seq))

    out = _merge(yn, osb, proj, x2, b_gate.reshape(1, 2 * D_MODEL), w_ssd_proj.astype(BF16),
                 w_sb_proj.astype(BF16), w_out.astype(BF16), norm_post.reshape(1, D_MODEL), tm)
    return out.reshape(bsz, seq, D_MODEL)


def kernel(x, norm_pre, w_in, b_gate, conv_w, conv_b, dt_bias, a_log, d_skip, ssd_norm,
           w_ssd_proj, w_sb_proj, w_out, norm_post):
    for layer in range(norm_pre.shape[0]):
        x = _layer(x, norm_pre[layer], w_in[layer], b_gate[layer], conv_w[layer],
                   conv_b[layer], dt_bias[layer], a_log[layer], d_skip[layer],
                   ssd_norm[layer], w_ssd_proj[layer], w_sb_proj[layer], w_out[layer],
                   norm_post[layer])
    return x
```

```python
import functools

import numpy as np
import jax
import jax.numpy as jnp
from jax import lax
from jax.experimental import pallas as pl
from jax.experimental.pallas import tpu as pltpu

F32 = jnp.float32
BF16 = jnp.bfloat16
HIGHEST = lax.Precision.HIGHEST

D_MODEL = 1024
EPS = 1e-6
LANES = 128

SSD_HEADS = 16
SSD_HEAD_DIM = 64
SSD_INNER = SSD_HEADS * SSD_HEAD_DIM
SSD_GROUPS = 2
SSD_STATE = 128
SSD_CONV = 4
HEADS_PER_GROUP = SSD_HEADS // SSD_GROUPS
GROUP_WIDTH = SSD_INNER // SSD_GROUPS
BC_WIDTH = 2 * SSD_GROUPS * SSD_STATE
CONV_DIM = SSD_INNER + BC_WIDTH
CONV_HALO = 8

SB_HEADS = 16
SB_HEAD_DIM = 64
SB_INNER = SB_HEADS * SB_HEAD_DIM
SB_BLOCK = 128
SB_QT = 256
SB_GROUP = 4
SB_NG = 2
SB_SATURATED = 104.5
SB_SOFTPLUS_LINEAR = 40.0

COL_ZSSD = 0
COL_Q = 1024
COL_K = 2048
COL_V = 3072
COL_ZSB = 4096
COL_GATE = 5120
COL_XS = 7168
COL_BC = 8192
PROJ_COLS = 8704

_SRC_ZSSD = 0
_SRC_XBC = _SRC_ZSSD + SSD_INNER
_SRC_DT = _SRC_XBC + CONV_DIM
_SRC_Q = _SRC_DT + SSD_HEADS
_SRC_K = _SRC_Q + SB_INNER
_SRC_V = _SRC_K + SB_INNER
_SRC_ZSB = _SRC_V + SB_INNER
_SRC_GATE = _SRC_ZSB + SB_INNER

VMEM_LIMIT = 56 * 1024 * 1024
INPROJ_TM = 1024
INPROJ_TN = 512


def _softplus(x):
    return jnp.maximum(x, 0.0) + jnp.log1p(jnp.exp(-jnp.abs(x)))


def _sigmoid(x):
    return 1.0 / (1.0 + jnp.exp(-x))


def _inproj_kernel(x_ref, nw_ref, w_ref, wdt_ref, proj_ref, dt_ref, h_ref):
    @pl.when(pl.program_id(1) == 0)
    def _():
        x = x_ref[...]
        ms = jnp.mean(x * x, axis=-1, keepdims=True)
        h = (x * lax.rsqrt(ms + EPS) * nw_ref[...]).astype(BF16)
        h_ref[...] = h
        dt_ref[...] = jnp.dot(h, wdt_ref[...], preferred_element_type=F32)

    w = w_ref[pl.program_id(1)]
    proj_ref[...] = jnp.dot(h_ref[...], w, preferred_element_type=F32).astype(BF16)


def _inproj(x2, norm_pre, w_r, w_dt, tm):
    m = x2.shape[0]
    nchunk, _, tn = w_r.shape
    return pl.pallas_call(
        _inproj_kernel,
        grid=(m // tm, nchunk),
        in_specs=[
            pl.BlockSpec((tm, D_MODEL), lambda i, j: (i, 0)),
            pl.BlockSpec((1, D_MODEL), lambda i, j: (0, 0)),
            pl.BlockSpec((nchunk, D_MODEL, tn), lambda i, j: (0, 0, 0)),
            pl.BlockSpec((D_MODEL, LANES), lambda i, j: (0, 0)),
        ],
        out_specs=[
            pl.BlockSpec((tm, tn), lambda i, j: (i, j)),
            pl.BlockSpec((tm, LANES), lambda i, j: (i, 0)),
        ],
        out_shape=[
            jax.ShapeDtypeStruct((m, PROJ_COLS), BF16),
            jax.ShapeDtypeStruct((m, LANES), F32),
        ],
        scratch_shapes=[pltpu.VMEM((tm, D_MODEL), BF16)],
        compiler_params=pltpu.CompilerParams(
            dimension_semantics=("parallel", "arbitrary"), vmem_limit_bytes=VMEM_LIMIT),
        name="inproj",
    )(x2, norm_pre, w_r, w_dt)


def _ssd_kernel(z_ref, xs_ref, bc_ref, dt_ref, cw_ref, cb_ref, dtb_ref, alog_ref, dsk_ref,
                nw_ref, yn_ref, state_ref, xpad_ref, y_ref, *, t):
    c = pl.program_id(1)

    @pl.when(c == 0)
    def _():
        state_ref[...] = jnp.zeros_like(state_ref)
        xpad_ref[pl.ds(t, CONV_HALO), :] = jnp.zeros((CONV_HALO, CONV_DIM), F32)

    xpad_ref[pl.ds(0, CONV_HALO), :] = xpad_ref[pl.ds(t, CONV_HALO), :]
    xpad_ref[pl.ds(CONV_HALO, t), pl.ds(0, SSD_INNER)] = xs_ref[...].astype(F32)
    xpad_ref[pl.ds(CONV_HALO, t), pl.ds(SSD_INNER, BC_WIDTH)] = bc_ref[...].astype(F32)
    conv = jnp.broadcast_to(cb_ref[...], (t, CONV_DIM))
    for j in range(SSD_CONV):
        conv = conv + cw_ref[pl.ds(j, 1), :] * xpad_ref[pl.ds(CONV_HALO - (SSD_CONV - 1) + j, t), :]
    xbc = conv * _sigmoid(conv)
    xs = xbc[:, :SSD_INNER]

    dt = _softplus(dt_ref[...] + dtb_ref[...])
    a_dt = dt * (-jnp.exp(alog_ref[...]))
    ri = lax.broadcasted_iota(jnp.int32, (t, t), 0)
    ci = lax.broadcasted_iota(jnp.int32, (t, t), 1)
    causal = ci <= ri
    a_cs = jnp.dot(causal.astype(F32), a_dt, preferred_element_type=F32, precision=HIGHEST)
    a_cs_t = a_cs.T

    ek = lax.broadcasted_iota(jnp.int32, (LANES, SSD_INNER), 0)
    ec = lax.broadcasted_iota(jnp.int32, (LANES, SSD_INNER), 1)
    expand = (ec // SSD_HEAD_DIM == ek).astype(F32)
    dt_e = jnp.dot(dt, expand, preferred_element_type=F32, precision=HIGHEST)
    acs_e = jnp.dot(a_cs, expand, preferred_element_type=F32, precision=HIGHEST)
    alast_e = acs_e[t - 1:t, :]
    x_dt = xs * dt_e
    x_dt_b = x_dt.astype(BF16)
    x_ds_b = (x_dt * jnp.exp(alast_e - acs_e)).astype(BF16)
    ecs_e = jnp.exp(acs_e)
    chunk_decay_e = jnp.exp(alast_e)

    lane = lax.broadcasted_iota(jnp.int32, (t, LANES), 1)
    first_head = lane < SSD_HEAD_DIM
    for g in range(SSD_GROUPS):
        b_g = xbc[:, SSD_INNER + g * SSD_STATE:SSD_INNER + (g + 1) * SSD_STATE].astype(BF16)
        c_g = xbc[:, SSD_INNER + (SSD_GROUPS + g) * SSD_STATE:
                  SSD_INNER + (SSD_GROUPS + g + 1) * SSD_STATE].astype(BF16)
        scores = lax.dot_general(c_g, b_g, (((1,), (1,)), ((), ())), preferred_element_type=F32)
        for pr in range(HEADS_PER_GROUP // 2):
            col0 = g * GROUP_WIDTH + pr * LANES
            x_pair = x_dt_b[:, col0:col0 + LANES]
            halves = []
            for hh in range(2):
                h = g * HEADS_PER_GROUP + pr * 2 + hh
                seg = a_cs[:, h:h + 1] - a_cs_t[h:h + 1, :]
                decay = jnp.exp(jnp.where(causal, seg, -jnp.inf))
                attn = (scores * decay).astype(BF16)
                halves.append(jnp.dot(attn, x_pair, preferred_element_type=F32))
            y_ref[:, pl.ds(col0, LANES)] = jnp.where(first_head, halves[0], halves[1])

        gs = pl.ds(g * GROUP_WIDTH, GROUP_WIDTH)
        st = state_ref[g]
        y_off = jnp.dot(c_g, st.astype(BF16), preferred_element_type=F32)
        y_ref[:, gs] = y_ref[:, gs] + y_off * ecs_e[:, g * GROUP_WIDTH:(g + 1) * GROUP_WIDTH]
        upd = lax.dot_general(b_g, x_ds_b[:, g * GROUP_WIDTH:(g + 1) * GROUP_WIDTH],
                              (((0,), (0,)), ((), ())), preferred_element_type=F32)
        state_ref[g] = st * chunk_decay_e[:, g * GROUP_WIDTH:(g + 1) * GROUP_WIDTH] + upd

    y = y_ref[...] + xs * dsk_ref[...]
    z = z_ref[...].astype(F32)
    gated = y * (z * _sigmoid(z))
    for g in range(SSD_GROUPS):
        gg = gated[:, g * GROUP_WIDTH:(g + 1) * GROUP_WIDTH]
        ms = jnp.mean(gg * gg, axis=-1, keepdims=True)
        yn_ref[:, pl.ds(g * GROUP_WIDTH, GROUP_WIDTH)] = (
            gg * lax.rsqrt(ms + EPS) * nw_ref[:, pl.ds(g * GROUP_WIDTH, GROUP_WIDTH)]).astype(BF16)


def _ssd(proj, dt_raw, conv_w, conv_b, dt_bias, a_log, d_skip_e, ssd_norm, bsz, seq, t):
    m = bsz * seq
    nc = seq // t
    row = lambda b, c: b * nc + c
    const = lambda b, c: (0, 0)
    return pl.pallas_call(
        functools.partial(_ssd_kernel, t=t),
        grid=(bsz, nc),
        in_specs=[
            pl.BlockSpec((t, SSD_INNER), lambda b, c: (row(b, c), COL_ZSSD // SSD_INNER)),
            pl.BlockSpec((t, SSD_INNER), lambda b, c: (row(b, c), COL_XS // SSD_INNER)),
            pl.BlockSpec((t, BC_WIDTH), lambda b, c: (row(b, c), COL_BC // BC_WIDTH)),
            pl.BlockSpec((t, LANES), lambda b, c: (row(b, c), 0)),
            pl.BlockSpec((SSD_CONV, CONV_DIM), const),
            pl.BlockSpec((1, CONV_DIM), const),
            pl.BlockSpec((1, LANES), const),
            pl.BlockSpec((1, LANES), const),
            pl.BlockSpec((1, SSD_INNER), const),
            pl.BlockSpec((1, SSD_INNER), const),
        ],
        out_specs=pl.BlockSpec((t, SSD_INNER), lambda b, c: (row(b, c), 0)),
        out_shape=jax.ShapeDtypeStruct((m, SSD_INNER), BF16),
        scratch_shapes=[
            pltpu.VMEM((SSD_GROUPS, SSD_STATE, GROUP_WIDTH), F32),
            pltpu.VMEM((t + CONV_HALO, CONV_DIM), F32),
            pltpu.VMEM((t, SSD_INNER), F32),
        ],
        compiler_params=pltpu.CompilerParams(
            dimension_semantics=("parallel", "arbitrary"), vmem_limit_bytes=VMEM_LIMIT),
        name="ssd",
    )(proj, proj, proj, dt_raw, conv_w, conv_b, dt_bias, a_log, d_skip_e, ssd_norm)


def _suffix_sum_matrix():
    j = np.arange(2 * SB_BLOCK)[:, None] % SB_BLOCK
    n = np.arange(2 * SB_BLOCK)[None, :]
    return np.where(n < SB_BLOCK, j > n, True).astype(np.float32)


def _sb_kernel(q_ref, k_ref, v_ref, zsb_ref, uu_ref, o_ref, qm_ref, acc_ref, car_ref, flag_ref,
               *, qt, ng):
    tile = pl.program_id(2)
    blk = SB_BLOCK
    gw = SB_GROUP * SB_HEAD_DIM
    q_head = lax.broadcasted_iota(jnp.int32, (qt, gw), 1) // SB_HEAD_DIM
    for g in range(ng):
        q = q_ref[:, pl.ds(g * gw, gw)] * jnp.asarray(SB_HEAD_DIM ** -0.5, BF16)
        for h in range(SB_GROUP):
            qm_ref[g * SB_GROUP + h] = jnp.where(q_head == h, q, jnp.zeros_like(q))
    acc_ref[...] = jnp.zeros_like(acc_ref)
    car_ref[...] = jnp.zeros_like(car_ref)
    flag_ref[0] = jnp.int32(0)
    uu = uu_ref[...]

    heads = [(g, h) for g in range(ng) for h in range(SB_GROUP)]

    def keep_mask(spec):
        j0, _, r0, nrows, diag_block = spec
        qi = tile * qt + r0 + lax.broadcasted_iota(jnp.int32, (nrows, blk), 0)
        ki = (j0 + diag_block) * blk + lax.broadcasted_iota(jnp.int32, (nrows, blk), 1)
        return ki < qi

    def key_rows(spec):
        j0, nblk = spec[0], spec[1]
        return pl.ds(pl.multiple_of(j0 * blk, blk), nblk * blk)

    def logits_stage(spec):
        rows = pl.ds(spec[2], spec[3])
        out = {}
        for g, h in heads:
            k2 = k_ref[key_rows(spec), pl.ds(g * gw, gw)]
            out[g, h] = lax.dot_general(
                qm_ref[g * SB_GROUP + h, rows, :], k2, (((1,), (1,)), ((), ())),
                preferred_element_type=F32)
        return out

    def sums_stage(spec, logit):
        nblk, diag_block = spec[1], spec[4]
        log_beta, sums = {}, {}
        for g, h in heads:
            s = logit[g, h]
            sp = jnp.where(s > SB_SOFTPLUS_LINEAR, s, jnp.log(1.0 + jnp.exp(s)))
            log_beta[g, h] = s - sp
            for c in range(nblk):
                x = sp[:, c * blk:(c + 1) * blk]
                if c == diag_block:
                    x = jnp.where(keep_mask(spec), x, 0.0)
                hi = x.astype(BF16)
                lo = (x - hi.astype(F32)).astype(BF16)
                sums[g, h, c] = jnp.dot(jnp.concatenate([hi, lo], axis=1), uu,
                                        preferred_element_type=F32)
        return log_beta, sums

    def output_stage(spec, log_beta, sums):
        nblk, diag_block = spec[1], spec[4]
        rows = pl.ds(spec[2], spec[3])
        v_head = lax.broadcasted_iota(jnp.int32, (nblk * blk, gw), 1) // SB_HEAD_DIM
        for g in range(ng):
            ws = []
            for h in range(SB_GROUP):
                hh = g * SB_GROUP + h
                car = car_ref[hh, rows, :]
                wh = [None] * nblk
                for c in reversed(range(nblk)):
                    cs = sums[g, h, c]
                    w = jnp.exp(log_beta[g, h][:, c * blk:(c + 1) * blk] - cs[:, :blk] - car)
                    if c == diag_block:
                        w = jnp.where(keep_mask(spec), w, 0.0)
                    wh[c] = w.astype(BF16)
                    car = car + cs[:, blk:]
                car_ref[hh, rows, :] = car
                ws.extend(wh)
            v2 = v_ref[key_rows(spec), pl.ds(g * gw, gw)]
            vm = jnp.concatenate(
                [jnp.where(v_head == h, v2, jnp.zeros_like(v2)) for h in range(SB_GROUP)], axis=0)
            pv = jnp.dot(jnp.concatenate(ws, axis=1), vm, preferred_element_type=F32)
            acc_ref[rows, pl.ds(g * gw, gw)] = acc_ref[rows, pl.ds(g * gw, gw)] + pv

    def run(specs):
        logits = [logits_stage(spec) for spec in specs]
        mids = [sums_stage(spec, logit) for spec, logit in zip(specs, logits)]
        for spec, (log_beta, sums) in zip(specs, mids):
            output_stage(spec, log_beta, sums)

    nsub = qt // blk
    diagonal = []
    for c in reversed(range(0, nsub, 2)):
        diagonal.append((tile * nsub + c, 2, (c + 1) * blk, qt - (c + 1) * blk, 1))
        diagonal.append((tile * nsub + c, 1, c * blk, blk, 0))

    def left_spec(it):
        return (tile * nsub - 2 * (it + 1), 2, 0, qt, None)

    def flag_saturation():
        flag_ref[0] = (jnp.min(car_ref[...]) >= SB_SATURATED).astype(jnp.int32)

    @pl.when(tile == 0)
    def _():
        run(diagonal)

    @pl.when(tile > 0)
    def _():
        run(diagonal + [left_spec(0)])
        flag_saturation()

    n_steps = tile * (nsub // 2)

    def more(it):
        return jnp.logical_and(it < n_steps, flag_ref[0] == 0)

    def body(it):
        run([left_spec(it)])
        flag_saturation()
        return it + 1

    lax.while_loop(more, body, jnp.int32(1))
    z = zsb_ref[...].astype(F32)
    o_ref[...] = (acc_ref[...] * (z * _sigmoid(z))).astype(BF16)


def _stick_breaking(proj, uu, bsz, seq, qt, ng):
    m = bsz * seq
    nq = seq // qt
    width = ng * SB_GROUP * SB_HEAD_DIM
    nh = ng * SB_GROUP
    return pl.pallas_call(
        functools.partial(_sb_kernel, qt=qt, ng=ng),
        grid=(bsz, SB_HEADS // nh, nq),
        in_specs=[
            pl.BlockSpec((qt, width), lambda b, p, i: (b * nq + i, COL_Q // width + p)),
            pl.BlockSpec((seq, width), lambda b, p, i: (b, COL_K // width + p)),
            pl.BlockSpec((seq, width), lambda b, p, i: (b, COL_V // width + p)),
            pl.BlockSpec((qt, width), lambda b, p, i: (b * nq + i, COL_ZSB // width + p)),
            pl.BlockSpec((2 * SB_BLOCK, 2 * SB_BLOCK), lambda b, p, i: (0, 0)),
        ],
        out_specs=pl.BlockSpec((qt, width), lambda b, p, i: (b * nq + i, p)),
        out_shape=jax.ShapeDtypeStruct((m, SB_INNER), BF16),
        scratch_shapes=[
            pltpu.VMEM((nh, qt, SB_GROUP * SB_HEAD_DIM), BF16),
            pltpu.VMEM((qt, width), F32),
            pltpu.VMEM((nh, qt, LANES), F32),
            pltpu.SMEM((1,), jnp.int32),
        ],
        compiler_params=pltpu.CompilerParams(
            dimension_semantics=("parallel", "parallel", "arbitrary"),
            vmem_limit_bytes=VMEM_LIMIT),
        name="stick_breaking",
    )(proj, proj, proj, proj, uu)


def _merge_kernel(yn_ref, osb_ref, g1_ref, g2_ref, x_ref, bg_ref, wssd_ref, wsb_ref, wout_ref,
                  nw_ref, out_ref):
    y_ssd = jnp.dot(yn_ref[...], wssd_ref[...], preferred_element_type=F32)
    y_sb = jnp.dot(osb_ref[...], wsb_ref[...], preferred_element_type=F32)
    g_ssd = _sigmoid(g1_ref[...].astype(F32) + bg_ref[:, pl.ds(0, D_MODEL)])
    g_sb = _sigmoid(g2_ref[...].astype(F32) + bg_ref[:, pl.ds(D_MODEL, D_MODEL)])
    merged = (g_ssd * y_ssd + g_sb * y_sb).astype(BF16)
    out = jnp.dot(merged, wout_ref[...], preferred_element_type=F32)
    ms = jnp.mean(out * out, axis=-1, keepdims=True)
    out_ref[...] = x_ref[...] + out * lax.rsqrt(ms + EPS) * nw_ref[...]


def _merge(yn, osb, proj, x2, b_gate, w_ssd, w_sb, w_out, norm_post, tm):
    m = x2.shape[0]
    rows = lambda i: (i, 0)
    const = lambda i: (0, 0)
    return pl.pallas_call(
        _merge_kernel,
        grid=(m // tm,),
        in_specs=[
            pl.BlockSpec((tm, D_MODEL), rows),
            pl.BlockSpec((tm, D_MODEL), rows),
            pl.BlockSpec((tm, D_MODEL), lambda i: (i, COL_GATE // D_MODEL)),
            pl.BlockSpec((tm, D_MODEL), lambda i: (i, COL_GATE // D_MODEL + 1)),
            pl.BlockSpec((tm, D_MODEL), rows),
            pl.BlockSpec((1, 2 * D_MODEL), const),
            pl.BlockSpec((D_MODEL, D_MODEL), const),
            pl.BlockSpec((D_MODEL, D_MODEL), const),
            pl.BlockSpec((D_MODEL, D_MODEL), const),
            pl.BlockSpec((1, D_MODEL), const),
        ],
        out_specs=pl.BlockSpec((tm, D_MODEL), rows),
        out_shape=jax.ShapeDtypeStruct((m, D_MODEL), F32),
        compiler_params=pltpu.CompilerParams(
            dimension_semantics=("parallel",), vmem_limit_bytes=VMEM_LIMIT),
        name="merge",
    )(yn, osb, proj, proj, x2, b_gate, w_ssd, w_sb, w_out, norm_post)


def _layer(x, norm_pre, w_in, b_gate, conv_w, conv_b, dt_bias, a_log, d_skip, ssd_norm,
           w_ssd_proj, w_sb_proj, w_out, norm_post):
    bsz, seq, _ = x.shape
    m = bsz * seq
    x2 = x.reshape(m, D_MODEL)

    seg = lambda start, width: w_in[:, start:start + width]
    w_r = jnp.concatenate([
        seg(_SRC_ZSSD, SSD_INNER), seg(_SRC_Q, SB_INNER), seg(_SRC_K, SB_INNER),
        seg(_SRC_V, SB_INNER), seg(_SRC_ZSB, SB_INNER), seg(_SRC_GATE, 2 * D_MODEL),
        seg(_SRC_XBC, CONV_DIM)], axis=1).astype(BF16)
    w_dt = jnp.pad(seg(_SRC_DT, SSD_HEADS), ((0, 0), (0, LANES - SSD_HEADS))).astype(BF16)
    pad_heads = lambda v: jnp.pad(v, (0, LANES - SSD_HEADS)).reshape(1, LANES)

    tm = min(512, m)
    w_chunks = w_r.reshape(D_MODEL, PROJ_COLS // INPROJ_TN, INPROJ_TN).transpose(1, 0, 2)
    proj, dt_raw = _inproj(x2, norm_pre.reshape(1, D_MODEL), w_chunks, w_dt, min(INPROJ_TM, m))

    t = min(128, seq)
    yn = _ssd(proj, dt_raw, conv_w, conv_b.reshape(1, CONV_DIM), pad_heads(dt_bias),
              pad_heads(a_log), jnp.repeat(d_skip, SSD_HEAD_DIM).reshape(1, SSD_INNER),
              ssd_norm.reshape(1, SSD_INNER), bsz, seq, t)

    osb = _stick_breaking(proj, jnp.asarray(_suffix_sum_matrix(), BF16), bsz, seq,
                          min(SB_QT, seq), SB_NG)

    out = _merge(yn, osb, proj, x2, b_gate.reshape(1, 2 * D_MODEL), w_ssd_proj.astype(BF16),
                 w_sb_proj.astype(BF16), w_out.astype(BF16), norm_post.reshape(1, D_MODEL), tm)
    return out.reshape(bsz, seq, D_MODEL)


def kernel(x, norm_pre, w_in, b_gate, conv_w, conv_b, dt_bias, a_log, d_skip, ssd_norm,
           w_ssd_proj, w_sb_proj, w_out, norm_post):
    for layer in range(norm_pre.shape[0]):
        x = _layer(x, norm_pre[layer], w_in[layer], b_gate[layer], conv_w[layer],
                   conv_b[layer], dt_bias[layer], a_log[layer], d_skip[layer],
                   ssd_norm[layer], w_ssd_proj[layer], w_sb_proj[layer], w_out[layer],
                   norm_post[layer])
    return x
```

```python
import functools

import numpy as np
import jax
import jax.numpy as jnp
from jax import lax
from jax.experimental import pallas as pl
from jax.experimental.pallas import tpu as pltpu

F32 = jnp.float32
BF16 = jnp.bfloat16
HIGHEST = lax.Precision.HIGHEST

D_MODEL = 1024
EPS = 1e-6
LANES = 128

SSD_HEADS = 16
SSD_HEAD_DIM = 64
SSD_INNER = SSD_HEADS * SSD_HEAD_DIM
SSD_GROUPS = 2
SSD_STATE = 128
SSD_CONV = 4
HEADS_PER_GROUP = SSD_HEADS // SSD_GROUPS
GROUP_WIDTH = SSD_INNER // SSD_GROUPS
BC_WIDTH = 2 * SSD_GROUPS * SSD_STATE
CONV_DIM = SSD_INNER + BC_WIDTH
SSD_CHUNK = 128
SSD_SUBCHUNKS = 2
CONV_HALO = 16

SB_HEADS = 16
SB_HEAD_DIM = 64
SB_INNER = SB_HEADS * SB_HEAD_DIM
SB_BLOCK = 128
SB_QT = 256
SB_GROUP = 4
SB_NG = 2
SB_SATURATED = 104.5
SB_SOFTPLUS_LINEAR = 40.0

COL_ZSSD = 0
COL_Q = 1024
COL_K = 2048
COL_V = 3072
COL_ZSB = 4096
COL_GATE = 5120
COL_XS = 7168
COL_BC = 8192
PROJ_COLS = 8704

_SRC_ZSSD = 0
_SRC_XBC = _SRC_ZSSD + SSD_INNER
_SRC_DT = _SRC_XBC + CONV_DIM
_SRC_Q = _SRC_DT + SSD_HEADS
_SRC_K = _SRC_Q + SB_INNER
_SRC_V = _SRC_K + SB_INNER
_SRC_ZSB = _SRC_V + SB_INNER
_SRC_GATE = _SRC_ZSB + SB_INNER

VMEM_LIMIT = 56 * 1024 * 1024
INPROJ_TM = 1024
INPROJ_TN = 512


def _softplus(x):
    return jnp.maximum(x, 0.0) + jnp.log1p(jnp.exp(-jnp.abs(x)))


def _sigmoid(x):
    return 1.0 / (1.0 + jnp.exp(-x))


def _silu(x):
    half = 0.5 * x
    return half + half * jnp.tanh(half)


def _inproj_kernel(x_ref, nw_ref, w_ref, wdt_ref, proj_ref, dt_ref, h_ref):
    @pl.when(pl.program_id(1) == 0)
    def _():
        x = x_ref[...]
        ms = jnp.mean(x * x, axis=-1, keepdims=True)
        h = (x * lax.rsqrt(ms + EPS) * nw_ref[...]).astype(BF16)
        h_ref[...] = h
        dt_ref[...] = jnp.dot(h, wdt_ref[...], preferred_element_type=F32)

    w = w_ref[pl.program_id(1)]
    proj_ref[...] = jnp.dot(h_ref[...], w, preferred_element_type=F32).astype(BF16)


def _inproj(x2, norm_pre, w_r, w_dt, tm):
    m = x2.shape[0]
    nchunk, _, tn = w_r.shape
    return pl.pallas_call(
        _inproj_kernel,
        grid=(m // tm, nchunk),
        in_specs=[
            pl.BlockSpec((tm, D_MODEL), lambda i, j: (i, 0)),
            pl.BlockSpec((1, D_MODEL), lambda i, j: (0, 0)),
            pl.BlockSpec((nchunk, D_MODEL, tn), lambda i, j: (0, 0, 0)),
            pl.BlockSpec((D_MODEL, LANES), lambda i, j: (0, 0)),
        ],
        out_specs=[
            pl.BlockSpec((tm, tn), lambda i, j: (i, j)),
            pl.BlockSpec((tm, LANES), lambda i, j: (i, 0)),
        ],
        out_shape=[
            jax.ShapeDtypeStruct((m, PROJ_COLS), BF16),
            jax.ShapeDtypeStruct((m, LANES), F32),
        ],
        scratch_shapes=[pltpu.VMEM((tm, D_MODEL), BF16)],
        compiler_params=pltpu.CompilerParams(
            dimension_semantics=("parallel", "arbitrary"), vmem_limit_bytes=VMEM_LIMIT),
        name="inproj",
    )(x2, norm_pre, w_r, w_dt)


def _split2(x):
    hi = x.astype(BF16)
    lo = (x - hi.astype(F32)).astype(BF16)
    return hi, lo


def _ssd_kernel(z_ref, xs_ref, bc_ref, dt_ref, cw_ref, cb_ref, dtb_ref, alog_ref, dsk_ref,
                nw_ref, yn_ref, state_ref, xe_ref, y_ref, *, t, nsub):
    c = pl.program_id(1)

    @pl.when(c == 0)
    def _():
        state_ref[...] = jnp.zeros_like(state_ref)
        xe_ref[pl.ds(nsub * t, CONV_HALO), :] = jnp.zeros((CONV_HALO, CONV_DIM), BF16)

    xe_ref[pl.ds(0, CONV_HALO), :] = xe_ref[pl.ds(nsub * t, CONV_HALO), :]
    xe_ref[pl.ds(CONV_HALO, nsub * t), pl.ds(0, SSD_INNER)] = xs_ref[...]
    xe_ref[pl.ds(CONV_HALO, nsub * t), pl.ds(SSD_INNER, BC_WIDTH)] = bc_ref[...]
    for sub in range(nsub):
        _ssd_subchunk(sub, t, z_ref, dt_ref, cw_ref, cb_ref, dtb_ref, alog_ref, dsk_ref, nw_ref,
                      yn_ref, state_ref, xe_ref, y_ref)


def _ssd_subchunk(sub, t, z_ref, dt_ref, cw_ref, cb_ref, dtb_ref, alog_ref, dsk_ref, nw_ref,
                  yn_ref, state_ref, xe_ref, y_ref):
    rows = pl.ds(sub * t, t)
    y_ref = y_ref.at[sub]
    xe = xe_ref[pl.ds(sub * t, t + CONV_HALO), :]
    out_row = lax.broadcasted_iota(jnp.int32, (t, t + CONV_HALO), 0)
    in_row = lax.broadcasted_iota(jnp.int32, (t, t + CONV_HALO), 1)
    conv = cb_ref[...] + cw_ref[pl.ds(SSD_CONV - 1, 1), :] * xe[CONV_HALO:, :].astype(F32)
    for j in range(SSD_CONV - 1):
        lag = SSD_CONV - 1 - j
        shift = jnp.where(in_row == out_row + (CONV_HALO - lag), 1.0, 0.0).astype(BF16)
        conv = conv + cw_ref[pl.ds(j, 1), :] * jnp.dot(shift, xe, preferred_element_type=F32)
    xbc = _silu(conv)
    xs = xbc[:, :SSD_INNER]

    dt = _softplus(dt_ref[rows, :] + dtb_ref[...])
    a_dt = dt * (-jnp.exp(alog_ref[...]))
    ri = lax.broadcasted_iota(jnp.int32, (t, t), 0)
    ci = lax.broadcasted_iota(jnp.int32, (t, t), 1)
    causal = ci <= ri
    tri = jnp.where(causal, 1.0, 0.0).astype(BF16)
    hi, lo = _split2(a_dt)
    lo2 = (a_dt - hi.astype(F32) - lo.astype(F32)).astype(BF16)
    a_cs = jnp.dot(jnp.concatenate([tri, tri, tri], axis=1),
                   jnp.concatenate([hi, lo, lo2], axis=0), preferred_element_type=F32)
    a_cs_t = a_cs.T
    a_last = a_cs[t - 1:t, :]

    ek = lax.broadcasted_iota(jnp.int32, (2 * LANES, SSD_INNER), 0) % LANES
    ec = lax.broadcasted_iota(jnp.int32, (2 * LANES, SSD_INNER), 1)
    expand2 = jnp.where(ec // SSD_HEAD_DIM == ek, 1.0, 0.0).astype(BF16)

    def expand(v):
        return jnp.dot(jnp.concatenate(_split2(v), axis=1), expand2, preferred_element_type=F32)

    x_dt = xs * expand(dt)
    x_dt_b = x_dt.astype(BF16)
    x_ds_b = (x_dt * jnp.exp(expand(a_last - a_cs))).astype(BF16)
    ecs_e = jnp.exp(expand(a_cs))
    chunk_decay_e = expand(jnp.exp(jnp.broadcast_to(a_last, (SSD_STATE, LANES))))

    lane = lax.broadcasted_iota(jnp.int32, (t, LANES), 1)
    first_head = lane < SSD_HEAD_DIM
    for g in range(SSD_GROUPS):
        b_f = xbc[:, SSD_INNER + g * SSD_STATE:SSD_INNER + (g + 1) * SSD_STATE]
        b_g = b_f.astype(BF16)
        b_t = b_f.T.astype(BF16)
        c_g = xbc[:, SSD_INNER + (SSD_GROUPS + g) * SSD_STATE:
                  SSD_INNER + (SSD_GROUPS + g + 1) * SSD_STATE].astype(BF16)
        scores = lax.dot_general(c_g, b_g, (((1,), (1,)), ((), ())), preferred_element_type=F32)
        for pr in range(HEADS_PER_GROUP // 2):
            col0 = g * GROUP_WIDTH + pr * LANES
            x_pair = x_dt_b[:, col0:col0 + LANES]
            halves = []
            for hh in range(2):
                h = g * HEADS_PER_GROUP + pr * 2 + hh
                seg = a_cs[:, h:h + 1] - a_cs_t[h:h + 1, :]
                decay = jnp.exp(jnp.where(causal, seg, -jnp.inf))
                attn = (scores * decay).astype(BF16)
                halves.append(jnp.dot(attn, x_pair, preferred_element_type=F32))
            y_ref[:, pl.ds(col0, LANES)] = jnp.where(first_head, halves[0], halves[1])

        gs = pl.ds(g * GROUP_WIDTH, GROUP_WIDTH)
        cols = slice(g * GROUP_WIDTH, (g + 1) * GROUP_WIDTH)
        st = state_ref[g]
        y_off = jnp.dot(c_g, st.astype(BF16), preferred_element_type=F32)
        y_ref[:, gs] = y_ref[:, gs] + y_off * ecs_e[:, cols]
        upd = jnp.dot(b_t, x_ds_b[:, cols], preferred_element_type=F32)
        state_ref[g] = st * chunk_decay_e[:, cols] + upd

    y = y_ref[...] + xs * dsk_ref[...]
    gated = y * _silu(z_ref[rows, :].astype(F32))
    for g in range(SSD_GROUPS):
        gg = gated[:, g * GROUP_WIDTH:(g + 1) * GROUP_WIDTH]
        ms = jnp.mean(gg * gg, axis=-1, keepdims=True)
        yn_ref[rows, pl.ds(g * GROUP_WIDTH, GROUP_WIDTH)] = (
            gg * lax.rsqrt(ms + EPS) * nw_ref[:, pl.ds(g * GROUP_WIDTH, GROUP_WIDTH)]).astype(BF16)


def _ssd(proj, dt_raw, conv_w, conv_b, dt_bias, a_log, d_skip_e, ssd_norm, bsz, seq, t, nsub):
    m = bsz * seq
    rows = t * nsub
    nc = seq // rows
    row = lambda b, c: b * nc + c
    const = lambda b, c: (0, 0)
    return pl.pallas_call(
        functools.partial(_ssd_kernel, t=t, nsub=nsub),
        grid=(bsz, nc),
        in_specs=[
            pl.BlockSpec((rows, SSD_INNER), lambda b, c: (row(b, c), COL_ZSSD // SSD_INNER)),
            pl.BlockSpec((rows, SSD_INNER), lambda b, c: (row(b, c), COL_XS // SSD_INNER)),
            pl.BlockSpec((rows, BC_WIDTH), lambda b, c: (row(b, c), COL_BC // BC_WIDTH)),
            pl.BlockSpec((rows, LANES), lambda b, c: (row(b, c), 0)),
            pl.BlockSpec((SSD_CONV, CONV_DIM), const),
            pl.BlockSpec((1, CONV_DIM), const),
            pl.BlockSpec((1, LANES), const),
            pl.BlockSpec((1, LANES), const),
            pl.BlockSpec((1, SSD_INNER), const),
            pl.BlockSpec((1, SSD_INNER), const),
        ],
        out_specs=pl.BlockSpec((rows, SSD_INNER), lambda b, c: (row(b, c), 0)),
        out_shape=jax.ShapeDtypeStruct((m, SSD_INNER), BF16),
        scratch_shapes=[
            pltpu.VMEM((SSD_GROUPS, SSD_STATE, GROUP_WIDTH), F32),
            pltpu.VMEM((rows + CONV_HALO, CONV_DIM), BF16),
            pltpu.VMEM((nsub, t, SSD_INNER), F32),
        ],
        compiler_params=pltpu.CompilerParams(
            dimension_semantics=("parallel", "arbitrary"), vmem_limit_bytes=VMEM_LIMIT),
        name="ssd",
    )(proj, proj, proj, dt_raw, conv_w, conv_b, dt_bias, a_log, d_skip_e, ssd_norm)


def _suffix_sum_matrix():
    j = np.arange(2 * SB_BLOCK)[:, None] % SB_BLOCK
    n = np.arange(2 * SB_BLOCK)[None, :]
    return np.where(n < SB_BLOCK, j > n, True).astype(np.float32)


def _sb_kernel(q_ref, k_ref, v_ref, zsb_ref, uu_ref, o_ref, qm_ref, acc_ref, car_ref, flag_ref,
               *, qt, ng):
    tile = pl.program_id(2)
    blk = SB_BLOCK
    gw = SB_GROUP * SB_HEAD_DIM
    q_head = lax.broadcasted_iota(jnp.int32, (qt, gw), 1) // SB_HEAD_DIM
    for g in range(ng):
        q = q_ref[:, pl.ds(g * gw, gw)] * jnp.asarray(SB_HEAD_DIM ** -0.5, BF16)
        for h in range(SB_GROUP):
            qm_ref[g * SB_GROUP + h] = jnp.where(q_head == h, q, jnp.zeros_like(q))
    acc_ref[...] = jnp.zeros_like(acc_ref)
    car_ref[...] = jnp.zeros_like(car_ref)
    flag_ref[0] = jnp.int32(0)
    uu = uu_ref[...]

    heads = [(g, h) for g in range(ng) for h in range(SB_GROUP)]

    def keep_mask(spec):
        j0, _, r0, nrows, diag_block = spec
        qi = tile * qt + r0 + lax.broadcasted_iota(jnp.int32, (nrows, blk), 0)
        ki = (j0 + diag_block) * blk + lax.broadcasted_iota(jnp.int32, (nrows, blk), 1)
        return ki < qi

    def key_rows(spec):
        j0, nblk = spec[0], spec[1]
        return pl.ds(pl.multiple_of(j0 * blk, blk), nblk * blk)

    def logits_stage(spec):
        rows = pl.ds(spec[2], spec[3])
        out = {}
        for g, h in heads:
            k2 = k_ref[key_rows(spec), pl.ds(g * gw, gw)]
            out[g, h] = lax.dot_general(
                qm_ref[g * SB_GROUP + h, rows, :], k2, (((1,), (1,)), ((), ())),
                preferred_element_type=F32)
        return out

    def sums_stage(spec, logit):
        nblk, diag_block = spec[1], spec[4]
        log_beta, sums = {}, {}
        for g, h in heads:
            s = logit[g, h]
            sp = jnp.where(s > SB_SOFTPLUS_LINEAR, s, jnp.log(1.0 + jnp.exp(s)))
            log_beta[g, h] = s - sp
            for c in range(nblk):
                x = sp[:, c * blk:(c + 1) * blk]
                if c == diag_block:
                    x = jnp.where(keep_mask(spec), x, 0.0)
                hi = x.astype(BF16)
                lo = (x - hi.astype(F32)).astype(BF16)
                sums[g, h, c] = jnp.dot(jnp.concatenate([hi, lo], axis=1), uu,
                                        preferred_element_type=F32)
        return log_beta, sums

    def output_stage(spec, log_beta, sums):
        nblk, diag_block = spec[1], spec[4]
        rows = pl.ds(spec[2], spec[3])
        v_head = lax.broadcasted_iota(jnp.int32, (nblk * blk, gw), 1) // SB_HEAD_DIM
        for g in range(ng):
            ws = []
            for h in range(SB_GROUP):
                hh = g * SB_GROUP + h
                car = car_ref[hh, rows, :]
                wh = [None] * nblk
                for c in reversed(range(nblk)):
                    cs = sums[g, h, c]
                    w = jnp.exp(log_beta[g, h][:, c * blk:(c + 1) * blk] - cs[:, :blk] - car)
                    if c == diag_block:
                        w = jnp.where(keep_mask(spec), w, 0.0)
                    wh[c] = w.astype(BF16)
                    car = car + cs[:, blk:]
                car_ref[hh, rows, :] = car
                ws.extend(wh)
            v2 = v_ref[key_rows(spec), pl.ds(g * gw, gw)]
            vm = jnp.concatenate(
                [jnp.where(v_head == h, v2, jnp.zeros_like(v2)) for h in range(SB_GROUP)], axis=0)
            pv = jnp.dot(jnp.concatenate(ws, axis=1), vm, preferred_element_type=F32)
            acc_ref[rows, pl.ds(g * gw, gw)] = acc_ref[rows, pl.ds(g * gw, gw)] + pv

    def run(specs):
        logits = [logits_stage(spec) for spec in specs]
        mids = [sums_stage(spec, logit) for spec, logit in zip(specs, logits)]
        for spec, (log_beta, sums) in zip(specs, mids):
            output_stage(spec, log_beta, sums)

    nsub = qt // blk
    diagonal = []
    for c in reversed(range(0, nsub, 2)):
        diagonal.append((tile * nsub + c, 2, (c + 1) * blk, qt - (c + 1) * blk, 1))
        diagonal.append((tile * nsub + c, 1, c * blk, blk, 0))

    def left_spec(it):
        return (tile * nsub - 2 * (it + 1), 2, 0, qt, None)

    def flag_saturation():
        flag_ref[0] = (jnp.min(car_ref[...]) >= SB_SATURATED).astype(jnp.int32)

    @pl.when(tile == 0)
    def _():
        run(diagonal)

    @pl.when(tile > 0)
    def _():
        run(diagonal + [left_spec(0)])
        flag_saturation()

    n_steps = tile * (nsub // 2)

    def more(it):
        return jnp.logical_and(it < n_steps, flag_ref[0] == 0)

    def body(it):
        run([left_spec(it)])
        flag_saturation()
        return it + 1

    lax.while_loop(more, body, jnp.int32(1))
    z = zsb_ref[...].astype(F32)
    o_ref[...] = (acc_ref[...] * (z * _sigmoid(z))).astype(BF16)


def _stick_breaking(proj, uu, bsz, seq, qt, ng):
    m = bsz * seq
    nq = seq // qt
    width = ng * SB_GROUP * SB_HEAD_DIM
    nh = ng * SB_GROUP
    return pl.pallas_call(
        functools.partial(_sb_kernel, qt=qt, ng=ng),
        grid=(bsz, SB_HEADS // nh, nq),
        in_specs=[
            pl.BlockSpec((qt, width), lambda b, p, i: (b * nq + i, COL_Q // width + p)),
            pl.BlockSpec((seq, width), lambda b, p, i: (b, COL_K // width + p)),
            pl.BlockSpec((seq, width), lambda b, p, i: (b, COL_V // width + p)),
            pl.BlockSpec((qt, width), lambda b, p, i: (b * nq + i, COL_ZSB // width + p)),
            pl.BlockSpec((2 * SB_BLOCK, 2 * SB_BLOCK), lambda b, p, i: (0, 0)),
        ],
        out_specs=pl.BlockSpec((qt, width), lambda b, p, i: (b * nq + i, p)),
        out_shape=jax.ShapeDtypeStruct((m, SB_INNER), BF16),
        scratch_shapes=[
            pltpu.VMEM((nh, qt, SB_GROUP * SB_HEAD_DIM), BF16),
            pltpu.VMEM((qt, width), F32),
            pltpu.VMEM((nh, qt, LANES), F32),
            pltpu.SMEM((1,), jnp.int32),
        ],
        compiler_params=pltpu.CompilerParams(
            dimension_semantics=("parallel", "parallel", "arbitrary"),
            vmem_limit_bytes=VMEM_LIMIT),
        name="stick_breaking",
    )(proj, proj, proj, proj, uu)


def _merge_kernel(yn_ref, osb_ref, g1_ref, g2_ref, x_ref, bg_ref, wssd_ref, wsb_ref, wout_ref,
                  nw_ref, out_ref):
    y_ssd = jnp.dot(yn_ref[...], wssd_ref[...], preferred_element_type=F32)
    y_sb = jnp.dot(osb_ref[...], wsb_ref[...], preferred_element_type=F32)
    g_ssd = _sigmoid(g1_ref[...].astype(F32) + bg_ref[:, pl.ds(0, D_MODEL)])
    g_sb = _sigmoid(g2_ref[...].astype(F32) + bg_ref[:, pl.ds(D_MODEL, D_MODEL)])
    merged = (g_ssd * y_ssd + g_sb * y_sb).astype(BF16)
    out = jnp.dot(merged, wout_ref[...], preferred_element_type=F32)
    ms = jnp.mean(out * out, axis=-1, keepdims=True)
    out_ref[...] = x_ref[...] + out * lax.rsqrt(ms + EPS) * nw_ref[...]


def _merge(yn, osb, proj, x2, b_gate, w_ssd, w_sb, w_out, norm_post, tm):
    m = x2.shape[0]
    rows = lambda i: (i, 0)
    const = lambda i: (0, 0)
    return pl.pallas_call(
        _merge_kernel,
        grid=(m // tm,),
        in_specs=[
            pl.BlockSpec((tm, D_MODEL), rows),
            pl.BlockSpec((tm, D_MODEL), rows),
            pl.BlockSpec((tm, D_MODEL), lambda i: (i, COL_GATE // D_MODEL)),
            pl.BlockSpec((tm, D_MODEL), lambda i: (i, COL_GATE // D_MODEL + 1)),
            pl.BlockSpec((tm, D_MODEL), rows),
            pl.BlockSpec((1, 2 * D_MODEL), const),
            pl.BlockSpec((D_MODEL, D_MODEL), const),
            pl.BlockSpec((D_MODEL, D_MODEL), const),
            pl.BlockSpec((D_MODEL, D_MODEL), const),
            pl.BlockSpec((1, D_MODEL), const),
        ],
        out_specs=pl.BlockSpec((tm, D_MODEL), rows),
        out_shape=jax.ShapeDtypeStruct((m, D_MODEL), F32),
        compiler_params=pltpu.CompilerParams(
            dimension_semantics=("parallel",), vmem_limit_bytes=VMEM_LIMIT),
        name="merge",
    )(yn, osb, proj, proj, x2, b_gate, w_ssd, w_sb, w_out, norm_post)


def _layer(x, norm_pre, w_in, b_gate, conv_w, conv_b, dt_bias, a_log, d_skip, ssd_norm,
           w_ssd_proj, w_sb_proj, w_out, norm_post):
    bsz, seq, _ = x.shape
    m = bsz * seq
    x2 = x.reshape(m, D_MODEL)

    segments = [(_SRC_ZSSD, SSD_INNER), (_SRC_Q, SB_INNER), (_SRC_K, SB_INNER), (_SRC_V, SB_INNER),
                (_SRC_ZSB, SB_INNER), (_SRC_GATE, 2 * D_MODEL), (_SRC_XBC, CONV_DIM)]
    starts = [s + o for s, width in segments for o in range(0, width, INPROJ_TN)]
    w_chunks = jnp.stack([w_in[:, s:s + INPROJ_TN].astype(BF16) for s in starts])
    w_dt = jnp.pad(w_in[:, _SRC_DT:_SRC_DT + SSD_HEADS],
                   ((0, 0), (0, LANES - SSD_HEADS))).astype(BF16)
    pad_heads = lambda v: jnp.pad(v, (0, LANES - SSD_HEADS)).reshape(1, LANES)

    tm = min(512, m)
    proj, dt_raw = _inproj(x2, norm_pre.reshape(1, D_MODEL), w_chunks, w_dt, min(INPROJ_TM, m))

    yn = _ssd(proj, dt_raw, conv_w, conv_b.reshape(1, CONV_DIM), pad_heads(dt_bias),
              pad_heads(a_log), jnp.repeat(d_skip, SSD_HEAD_DIM).reshape(1, SSD_INNER),
              ssd_norm.reshape(1, SSD_INNER), bsz, seq, SSD_CHUNK,
              min(SSD_SUBCHUNKS, seq // SSD_CHUNK))

    osb = _stick_breaking(proj, jnp.asarray(_suffix_sum_matrix(), BF16), bsz, seq,
                          min(SB_QT, seq), SB_NG)

    out = _merge(yn, osb, proj, x2, b_gate.reshape(1, 2 * D_MODEL), w_ssd_proj.astype(BF16),
                 w_sb_proj.astype(BF16), w_out.astype(BF16), norm_post.reshape(1, D_MODEL), tm)
    return out.reshape(bsz, seq, D_MODEL)


def kernel(x, norm_pre, w_in, b_gate, conv_w, conv_b, dt_bias, a_log, d_skip, ssd_norm,
           w_ssd_proj, w_sb_proj, w_out, norm_post):
    for layer in range(norm_pre.shape[0]):
        x = _layer(x, norm_pre[layer], w_in[layer], b_gate[layer], conv_w[layer],
                   conv_b[layer], dt_bias[layer], a_log[layer], d_skip[layer],
                   ssd_norm[layer], w_ssd_proj[layer], w_sb_proj[layer], w_out[layer],
                   norm_post[layer])
    return x
```

```python
import functools

import numpy as np
import jax
import jax.numpy as jnp
from jax import lax
from jax.experimental import pallas as pl
from jax.experimental.pallas import tpu as pltpu

F32 = jnp.float32
BF16 = jnp.bfloat16
HIGHEST = lax.Precision.HIGHEST

D_MODEL = 1024
EPS = 1e-6
LANES = 128

SSD_HEADS = 16
SSD_HEAD_DIM = 64
SSD_INNER = SSD_HEADS * SSD_HEAD_DIM
SSD_GROUPS = 2
SSD_STATE = 128
SSD_CONV = 4
HEADS_PER_GROUP = SSD_HEADS // SSD_GROUPS
GROUP_WIDTH = SSD_INNER // SSD_GROUPS
BC_WIDTH = 2 * SSD_GROUPS * SSD_STATE
CONV_DIM = SSD_INNER + BC_WIDTH
SSD_CHUNK = 128
SSD_SUBCHUNKS = 2
CONV_HALO = 16

SB_HEADS = 16
SB_HEAD_DIM = 64
SB_INNER = SB_HEADS * SB_HEAD_DIM
SB_BLOCK = 128
SB_QT = 256
SB_GROUP = 4
SB_NG = 2
SB_SATURATED = 104.5
SB_SOFTPLUS_LINEAR = 40.0

COL_ZSSD = 0
COL_XS = COL_ZSSD + SSD_INNER
COL_BC = COL_XS + SSD_INNER
COL_Q = COL_BC + BC_WIDTH
COL_K = COL_Q + SB_INNER
COL_V = COL_K + SB_INNER
COL_ZSB = COL_V + SB_INNER
COL_GATE = COL_ZSB + SB_INNER
PROJ_COLS = COL_GATE + 2 * D_MODEL
_SRC_DT = COL_Q
PROJ_CHUNK = 512

VMEM_LIMIT = 56 * 1024 * 1024
INPROJ_TM = 1024


def _softplus(x):
    return jnp.maximum(x, 0.0) + jnp.log1p(jnp.exp(-jnp.abs(x)))


def _sigmoid(x):
    return 1.0 / (1.0 + jnp.exp(-x))


def _silu(x):
    half = 0.5 * x
    return half + half * jnp.tanh(half)


def _inproj_kernel(x_ref, nw_ref, w_ref, wdt_ref, proj_ref, dt_ref, h_ref):
    @pl.when(pl.program_id(1) == 0)
    def _():
        x = x_ref[...]
        ms = jnp.mean(x * x, axis=-1, keepdims=True)
        h = (x * lax.rsqrt(ms + EPS) * nw_ref[...]).astype(BF16)
        h_ref[...] = h
        dt_ref[...] = jnp.dot(h, wdt_ref[...], preferred_element_type=F32)

    tn = PROJ_CHUNK
    w = w_ref[:, pl.ds(pl.multiple_of(pl.program_id(1) * tn, tn), tn)]
    proj_ref[...] = jnp.dot(h_ref[...], w, preferred_element_type=F32).astype(BF16)


def _inproj(x2, norm_pre, w_r, w_dt, tm):
    m = x2.shape[0]
    tn = PROJ_CHUNK
    return pl.pallas_call(
        _inproj_kernel,
        grid=(m // tm, PROJ_COLS // tn),
        in_specs=[
            pl.BlockSpec((tm, D_MODEL), lambda i, j: (i, 0)),
            pl.BlockSpec((1, D_MODEL), lambda i, j: (0, 0)),
            pl.BlockSpec((D_MODEL, PROJ_COLS), lambda i, j: (0, 0)),
            pl.BlockSpec((D_MODEL, LANES), lambda i, j: (0, 0)),
        ],
        out_specs=[
            pl.BlockSpec((None, tm, tn), lambda i, j: (j, i, 0)),
            pl.BlockSpec((tm, LANES), lambda i, j: (i, 0)),
        ],
        out_shape=[
            jax.ShapeDtypeStruct((PROJ_COLS // tn, m, tn), BF16),
            jax.ShapeDtypeStruct((m, LANES), F32),
        ],
        scratch_shapes=[pltpu.VMEM((tm, D_MODEL), BF16)],
        compiler_params=pltpu.CompilerParams(
            dimension_semantics=("parallel", "arbitrary"), vmem_limit_bytes=VMEM_LIMIT),
        name="inproj",
    )(x2, norm_pre, w_r, w_dt)


def _split2(x):
    hi = x.astype(BF16)
    lo = (x - hi.astype(F32)).astype(BF16)
    return hi, lo


def _ssd_kernel(z0_ref, z1_ref, xs0_ref, xs1_ref, bc_ref, dt_ref, cw_ref, cb_ref, dtb_ref,
                alog_ref, dsk_ref, nw_ref, yn_ref, state_ref, xe_ref, y_ref, *, t, nsub):
    c = pl.program_id(1)

    @pl.when(c == 0)
    def _():
        state_ref[...] = jnp.zeros_like(state_ref)
        xe_ref[pl.ds(nsub * t, CONV_HALO), :] = jnp.zeros((CONV_HALO, CONV_DIM), BF16)

    xe_ref[pl.ds(0, CONV_HALO), :] = xe_ref[pl.ds(nsub * t, CONV_HALO), :]
    for k, ref in enumerate((xs0_ref, xs1_ref, bc_ref)):
        xe_ref[pl.ds(CONV_HALO, nsub * t), pl.ds(k * PROJ_CHUNK, PROJ_CHUNK)] = ref[...]
    for sub in range(nsub):
        _ssd_subchunk(sub, t, (z0_ref, z1_ref), dt_ref, cw_ref, cb_ref, dtb_ref, alog_ref,
                      dsk_ref, nw_ref, yn_ref, state_ref, xe_ref, y_ref)


def _ssd_subchunk(sub, t, z_refs, dt_ref, cw_ref, cb_ref, dtb_ref, alog_ref, dsk_ref, nw_ref,
                  yn_ref, state_ref, xe_ref, y_ref):
    rows = pl.ds(sub * t, t)
    y_ref = y_ref.at[sub]
    xe = xe_ref[pl.ds(sub * t, t + CONV_HALO), :]
    out_row = lax.broadcasted_iota(jnp.int32, (t, t + CONV_HALO), 0)
    in_row = lax.broadcasted_iota(jnp.int32, (t, t + CONV_HALO), 1)
    conv = cb_ref[...] + cw_ref[pl.ds(SSD_CONV - 1, 1), :] * xe[CONV_HALO:, :].astype(F32)
    for j in range(SSD_CONV - 1):
        lag = SSD_CONV - 1 - j
        shift = jnp.where(in_row == out_row + (CONV_HALO - lag), 1.0, 0.0).astype(BF16)
        conv = conv + cw_ref[pl.ds(j, 1), :] * jnp.dot(shift, xe, preferred_element_type=F32)
    xbc = _silu(conv)
    xs = xbc[:, :SSD_INNER]

    dt = _softplus(dt_ref[rows, :] + dtb_ref[...])
    a_dt = dt * (-jnp.exp(alog_ref[...]))
    ri = lax.broadcasted_iota(jnp.int32, (t, t), 0)
    ci = lax.broadcasted_iota(jnp.int32, (t, t), 1)
    causal = ci <= ri
    tri = jnp.where(causal, 1.0, 0.0).astype(BF16)
    hi, lo = _split2(a_dt)
    lo2 = (a_dt - hi.astype(F32) - lo.astype(F32)).astype(BF16)
    a_cs = jnp.dot(jnp.concatenate([tri, tri, tri], axis=1),
                   jnp.concatenate([hi, lo, lo2], axis=0), preferred_element_type=F32)
    a_cs_t = a_cs.T
    a_last = a_cs[t - 1:t, :]

    ek = lax.broadcasted_iota(jnp.int32, (2 * LANES, SSD_INNER), 0) % LANES
    ec = lax.broadcasted_iota(jnp.int32, (2 * LANES, SSD_INNER), 1)
    expand2 = jnp.where(ec // SSD_HEAD_DIM == ek, 1.0, 0.0).astype(BF16)

    def expand(v):
        return jnp.dot(jnp.concatenate(_split2(v), axis=1), expand2, preferred_element_type=F32)

    x_dt = xs * expand(dt)
    x_dt_b = x_dt.astype(BF16)
    x_ds_b = (x_dt * jnp.exp(expand(a_last - a_cs))).astype(BF16)
    ecs_e = jnp.exp(expand(a_cs))
    chunk_decay_e = expand(jnp.exp(jnp.broadcast_to(a_last, (SSD_STATE, LANES))))

    lane = lax.broadcasted_iota(jnp.int32, (t, LANES), 1)
    first_head = lane < SSD_HEAD_DIM
    for g in range(SSD_GROUPS):
        b_f = xbc[:, SSD_INNER + g * SSD_STATE:SSD_INNER + (g + 1) * SSD_STATE]
        b_g = b_f.astype(BF16)
        b_t = b_f.T.astype(BF16)
        c_g = xbc[:, SSD_INNER + (SSD_GROUPS + g) * SSD_STATE:
                  SSD_INNER + (SSD_GROUPS + g + 1) * SSD_STATE].astype(BF16)
        scores = lax.dot_general(c_g, b_g, (((1,), (1,)), ((), ())), preferred_element_type=F32)
        for pr in range(HEADS_PER_GROUP // 2):
            col0 = g * GROUP_WIDTH + pr * LANES
            x_pair = x_dt_b[:, col0:col0 + LANES]
            halves = []
            for hh in range(2):
                h = g * HEADS_PER_GROUP + pr * 2 + hh
                seg = a_cs[:, h:h + 1] - a_cs_t[h:h + 1, :]
                decay = jnp.exp(jnp.where(causal, seg, -jnp.inf))
                attn = (scores * decay).astype(BF16)
                halves.append(jnp.dot(attn, x_pair, preferred_element_type=F32))
            y_ref[:, pl.ds(col0, LANES)] = jnp.where(first_head, halves[0], halves[1])

        gs = pl.ds(g * GROUP_WIDTH, GROUP_WIDTH)
        cols = slice(g * GROUP_WIDTH, (g + 1) * GROUP_WIDTH)
        st = state_ref[g]
        y_off = jnp.dot(c_g, st.astype(BF16), preferred_element_type=F32)
        y_ref[:, gs] = y_ref[:, gs] + y_off * ecs_e[:, cols]
        upd = jnp.dot(b_t, x_ds_b[:, cols], preferred_element_type=F32)
        state_ref[g] = st * chunk_decay_e[:, cols] + upd

    y = y_ref[...] + xs * dsk_ref[...]
    z = jnp.concatenate([ref[rows, :] for ref in z_refs], axis=1).astype(F32)
    gated = y * _silu(z)
    for g in range(SSD_GROUPS):
        gg = gated[:, g * GROUP_WIDTH:(g + 1) * GROUP_WIDTH]
        ms = jnp.mean(gg * gg, axis=-1, keepdims=True)
        yn_ref[rows, pl.ds(g * GROUP_WIDTH, GROUP_WIDTH)] = (
            gg * lax.rsqrt(ms + EPS) * nw_ref[:, pl.ds(g * GROUP_WIDTH, GROUP_WIDTH)]).astype(BF16)


def _ssd(proj, dt_raw, conv_w, conv_b, dt_bias, a_log, d_skip_e, ssd_norm, bsz, seq, t, nsub):
    m = bsz * seq
    rows = t * nsub
    nc = seq // rows
    row = lambda b, c: b * nc + c
    const = lambda b, c: (0, 0)
    chunk = lambda col: pl.BlockSpec(
        (None, rows, PROJ_CHUNK), lambda b, c: (col // PROJ_CHUNK, row(b, c), 0))
    return pl.pallas_call(
        functools.partial(_ssd_kernel, t=t, nsub=nsub),
        grid=(bsz, nc),
        in_specs=[
            chunk(COL_ZSSD), chunk(COL_ZSSD + PROJ_CHUNK),
            chunk(COL_XS), chunk(COL_XS + PROJ_CHUNK), chunk(COL_BC),
            pl.BlockSpec((rows, LANES), lambda b, c: (row(b, c), 0)),
            pl.BlockSpec((SSD_CONV, CONV_DIM), const),
            pl.BlockSpec((1, CONV_DIM), const),
            pl.BlockSpec((1, LANES), const),
            pl.BlockSpec((1, LANES), const),
            pl.BlockSpec((1, SSD_INNER), const),
            pl.BlockSpec((1, SSD_INNER), const),
        ],
        out_specs=pl.BlockSpec((rows, SSD_INNER), lambda b, c: (row(b, c), 0)),
        out_shape=jax.ShapeDtypeStruct((m, SSD_INNER), BF16),
        scratch_shapes=[
            pltpu.VMEM((SSD_GROUPS, SSD_STATE, GROUP_WIDTH), F32),
            pltpu.VMEM((rows + CONV_HALO, CONV_DIM), BF16),
            pltpu.VMEM((nsub, t, SSD_INNER), F32),
        ],
        compiler_params=pltpu.CompilerParams(
            dimension_semantics=("parallel", "arbitrary"), vmem_limit_bytes=VMEM_LIMIT),
        name="ssd",
    )(proj, proj, proj, proj, proj, dt_raw, conv_w, conv_b, dt_bias, a_log, d_skip_e, ssd_norm)


def _suffix_sum_matrix():
    j = np.arange(2 * SB_BLOCK)[:, None] % SB_BLOCK
    n = np.arange(2 * SB_BLOCK)[None, :]
    return np.where(n < SB_BLOCK, j > n, True).astype(np.float32)


def _sb_kernel(q_ref, k_ref, v_ref, zsb_ref, uu_ref, o_ref, qm_ref, acc_ref, car_ref, flag_ref,
               *, qt, ng):
    tile = pl.program_id(2)
    blk = SB_BLOCK
    gw = SB_GROUP * SB_HEAD_DIM
    q_head = lax.broadcasted_iota(jnp.int32, (qt, gw), 1) // SB_HEAD_DIM
    for g in range(ng):
        q = q_ref[:, pl.ds(g * gw, gw)] * jnp.asarray(SB_HEAD_DIM ** -0.5, BF16)
        for h in range(SB_GROUP):
            qm_ref[g * SB_GROUP + h] = jnp.where(q_head == h, q, jnp.zeros_like(q))
    acc_ref[...] = jnp.zeros_like(acc_ref)
    car_ref[...] = jnp.zeros_like(car_ref)
    flag_ref[0] = jnp.int32(0)
    uu = uu_ref[...]

    heads = [(g, h) for g in range(ng) for h in range(SB_GROUP)]

    def keep_mask(spec):
        j0, _, r0, nrows, diag_block = spec
        qi = tile * qt + r0 + lax.broadcasted_iota(jnp.int32, (nrows, blk), 0)
        ki = (j0 + diag_block) * blk + lax.broadcasted_iota(jnp.int32, (nrows, blk), 1)
        return ki < qi

    def key_rows(spec):
        j0, nblk = spec[0], spec[1]
        return pl.ds(pl.multiple_of(j0 * blk, blk), nblk * blk)

    def logits_stage(spec):
        rows = pl.ds(spec[2], spec[3])
        out = {}
        for g, h in heads:
            k2 = k_ref[key_rows(spec), pl.ds(g * gw, gw)]
            out[g, h] = lax.dot_general(
                qm_ref[g * SB_GROUP + h, rows, :], k2, (((1,), (1,)), ((), ())),
                preferred_element_type=F32)
        return out

    def sums_stage(spec, logit):
        nblk, diag_block = spec[1], spec[4]
        log_beta, sums = {}, {}
        for g, h in heads:
            s = logit[g, h]
            sp = jnp.where(s > SB_SOFTPLUS_LINEAR, s, jnp.log(1.0 + jnp.exp(s)))
            log_beta[g, h] = s - sp
            for c in range(nblk):
                x = sp[:, c * blk:(c + 1) * blk]
                if c == diag_block:
                    x = jnp.where(keep_mask(spec), x, 0.0)
                hi = x.astype(BF16)
                lo = (x - hi.astype(F32)).astype(BF16)
                sums[g, h, c] = jnp.dot(jnp.concatenate([hi, lo], axis=1), uu,
                                        preferred_element_type=F32)
        return log_beta, sums

    def output_stage(spec, log_beta, sums):
        nblk, diag_block = spec[1], spec[4]
        rows = pl.ds(spec[2], spec[3])
        v_head = lax.broadcasted_iota(jnp.int32, (nblk * blk, gw), 1) // SB_HEAD_DIM
        for g in range(ng):
            ws = []
            for h in range(SB_GROUP):
                hh = g * SB_GROUP + h
                car = car_ref[hh, rows, :]
                wh = [None] * nblk
                for c in reversed(range(nblk)):
                    cs = sums[g, h, c]
                    w = jnp.exp(log_beta[g, h][:, c * blk:(c + 1) * blk] - cs[:, :blk] - car)
                    if c == diag_block:
                        w = jnp.where(keep_mask(spec), w, 0.0)
                    wh[c] = w.astype(BF16)
                    car = car + cs[:, blk:]
                car_ref[hh, rows, :] = car
                ws.extend(wh)
            v2 = v_ref[key_rows(spec), pl.ds(g * gw, gw)]
            vm = jnp.concatenate(
                [jnp.where(v_head == h, v2, jnp.zeros_like(v2)) for h in range(SB_GROUP)], axis=0)
            pv = jnp.dot(jnp.concatenate(ws, axis=1), vm, preferred_element_type=F32)
            acc_ref[rows, pl.ds(g * gw, gw)] = acc_ref[rows, pl.ds(g * gw, gw)] + pv

    def run(specs):
        logits = [logits_stage(spec) for spec in specs]
        mids = [sums_stage(spec, logit) for spec, logit in zip(specs, logits)]
        for spec, (log_beta, sums) in zip(specs, mids):
            output_stage(spec, log_beta, sums)

    nsub = qt // blk
    diagonal = []
    for c in reversed(range(0, nsub, 2)):
        diagonal.append((tile * nsub + c, 2, (c + 1) * blk, qt - (c + 1) * blk, 1))
        diagonal.append((tile * nsub + c, 1, c * blk, blk, 0))

    def left_spec(it):
        return (tile * nsub - 2 * (it + 1), 2, 0, qt, None)

    def flag_saturation():
        flag_ref[0] = (jnp.min(car_ref[...]) >= SB_SATURATED).astype(jnp.int32)

    @pl.when(tile == 0)
    def _():
        run(diagonal)

    @pl.when(tile > 0)
    def _():
        run(diagonal + [left_spec(0)])
        flag_saturation()

    n_steps = tile * (nsub // 2)

    def more(it):
        return jnp.logical_and(it < n_steps, flag_ref[0] == 0)

    def body(it):
        run([left_spec(it)])
        flag_saturation()
        return it + 1

    lax.while_loop(more, body, jnp.int32(1))
    z = zsb_ref[...].astype(F32)
    o_ref[...] = (acc_ref[...] * (z * _sigmoid(z))).astype(BF16)


def _stick_breaking(proj, uu, bsz, seq, qt, ng):
    m = bsz * seq
    nq = seq // qt
    width = ng * SB_GROUP * SB_HEAD_DIM
    nh = ng * SB_GROUP
    assert width == PROJ_CHUNK
    return pl.pallas_call(
        functools.partial(_sb_kernel, qt=qt, ng=ng),
        grid=(bsz, SB_HEADS // nh, nq),
        in_specs=[
            pl.BlockSpec((None, qt, width), lambda b, p, i: (COL_Q // width + p, b * nq + i, 0)),
            pl.BlockSpec((None, seq, width), lambda b, p, i: (COL_K // width + p, b, 0)),
            pl.BlockSpec((None, seq, width), lambda b, p, i: (COL_V // width + p, b, 0)),
            pl.BlockSpec((None, qt, width), lambda b, p, i: (COL_ZSB // width + p, b * nq + i, 0)),
            pl.BlockSpec((2 * SB_BLOCK, 2 * SB_BLOCK), lambda b, p, i: (0, 0)),
        ],
        out_specs=pl.BlockSpec((qt, width), lambda b, p, i: (b * nq + i, p)),
        out_shape=jax.ShapeDtypeStruct((m, SB_INNER), BF16),
        scratch_shapes=[
            pltpu.VMEM((nh, qt, SB_GROUP * SB_HEAD_DIM), BF16),
            pltpu.VMEM((qt, width), F32),
            pltpu.VMEM((nh, qt, LANES), F32),
            pltpu.SMEM((1,), jnp.int32),
        ],
        compiler_params=pltpu.CompilerParams(
            dimension_semantics=("parallel", "parallel", "arbitrary"),
            vmem_limit_bytes=VMEM_LIMIT),
        name="stick_breaking",
    )(proj, proj, proj, proj, uu)


def _merge_kernel(yn_ref, osb_ref, g0_ref, g1_ref, g2_ref, g3_ref, x_ref, bg_ref, wssd_ref,
                  wsb_ref, wout_ref, nw_ref, out_ref):
    y_ssd = jnp.dot(yn_ref[...], wssd_ref[...], preferred_element_type=F32)
    y_sb = jnp.dot(osb_ref[...], wsb_ref[...], preferred_element_type=F32)
    gates = jnp.concatenate([g0_ref[...], g1_ref[...], g2_ref[...], g3_ref[...]], axis=1)
    g = _sigmoid(gates.astype(F32) + bg_ref[...])
    merged = (g[:, :D_MODEL] * y_ssd + g[:, D_MODEL:] * y_sb).astype(BF16)
    out = jnp.dot(merged, wout_ref[...], preferred_element_type=F32)
    ms = jnp.mean(out * out, axis=-1, keepdims=True)
    out_ref[...] = x_ref[...] + out * lax.rsqrt(ms + EPS) * nw_ref[...]


def _merge(yn, osb, proj, x2, b_gate, w_ssd, w_sb, w_out, norm_post, tm):
    m = x2.shape[0]
    rows = lambda i: (i, 0)
    const = lambda i: (0, 0)
    gate = lambda k: pl.BlockSpec(
        (None, tm, PROJ_CHUNK), lambda i: (COL_GATE // PROJ_CHUNK + k, i, 0))
    return pl.pallas_call(
        _merge_kernel,
        grid=(m // tm,),
        in_specs=[
            pl.BlockSpec((tm, D_MODEL), rows),
            pl.BlockSpec((tm, D_MODEL), rows),
            gate(0), gate(1), gate(2), gate(3),
            pl.BlockSpec((tm, D_MODEL), rows),
            pl.BlockSpec((1, 2 * D_MODEL), const),
            pl.BlockSpec((D_MODEL, D_MODEL), const),
            pl.BlockSpec((D_MODEL, D_MODEL), const),
            pl.BlockSpec((D_MODEL, D_MODEL), const),
            pl.BlockSpec((1, D_MODEL), const),
        ],
        out_specs=pl.BlockSpec((tm, D_MODEL), rows),
        out_shape=jax.ShapeDtypeStruct((m, D_MODEL), F32),
        compiler_params=pltpu.CompilerParams(
            dimension_semantics=("parallel",), vmem_limit_bytes=VMEM_LIMIT),
        name="merge",
    )(yn, osb, proj, proj, proj, proj, x2, b_gate, w_ssd, w_sb, w_out, norm_post)


def _layer(x, norm_pre, w_in, b_gate, conv_w, conv_b, dt_bias, a_log, d_skip, ssd_norm,
           w_ssd_proj, w_sb_proj, w_out, norm_post):
    bsz, seq, _ = x.shape
    m = bsz * seq
    x2 = x.reshape(m, D_MODEL)

    w_r = jnp.concatenate([w_in[:, :_SRC_DT], w_in[:, _SRC_DT + SSD_HEADS:]], axis=1).astype(BF16)
    w_dt = jnp.pad(w_in[:, _SRC_DT:_SRC_DT + SSD_HEADS],
                   ((0, 0), (0, LANES - SSD_HEADS))).astype(BF16)
    pad_heads = lambda v: jnp.pad(v, (0, LANES - SSD_HEADS)).reshape(1, LANES)

    tm = min(512, m)
    proj, dt_raw = _inproj(x2, norm_pre.reshape(1, D_MODEL), w_r, w_dt, min(INPROJ_TM, m))

    yn = _ssd(proj, dt_raw, conv_w, conv_b.reshape(1, CONV_DIM), pad_heads(dt_bias),
              pad_heads(a_log), jnp.repeat(d_skip, SSD_HEAD_DIM).reshape(1, SSD_INNER),
              ssd_norm.reshape(1, SSD_INNER), bsz, seq, SSD_CHUNK,
              min(SSD_SUBCHUNKS, seq // SSD_CHUNK))

    osb = _stick_breaking(proj, jnp.asarray(_suffix_sum_matrix(), BF16), bsz, seq,
                          min(SB_QT, seq), SB_NG)

    out = _merge(yn, osb, proj, x2, b_gate.reshape(1, 2 * D_MODEL), w_ssd_proj.astype(BF16),
                 w_sb_proj.astype(BF16), w_out.astype(BF16), norm_post.reshape(1, D_MODEL), tm)
    return out.reshape(bsz, seq, D_MODEL)


def kernel(x, norm_pre, w_in, b_gate, conv_w, conv_b, dt_bias, a_log, d_skip, ssd_norm,
           w_ssd_proj, w_sb_proj, w_out, norm_post):
    for layer in range(norm_pre.shape[0]):
        x = _layer(x, norm_pre[layer], w_in[layer], b_gate[layer], conv_w[layer],
                   conv_b[layer], dt_bias[layer], a_log[layer], d_skip[layer],
                   ssd_norm[layer], w_ssd_proj[layer], w_sb_proj[layer], w_out[layer],
                   norm_post[layer])
    return x
```

```python
import functools

import numpy as np
import jax
import jax.numpy as jnp
from jax import lax
from jax.experimental import pallas as pl
from jax.experimental.pallas import tpu as pltpu

F32 = jnp.float32
BF16 = jnp.bfloat16
HIGHEST = lax.Precision.HIGHEST

D_MODEL = 1024
EPS = 1e-6
LANES = 128

SSD_HEADS = 16
SSD_HEAD_DIM = 64
SSD_INNER = SSD_HEADS * SSD_HEAD_DIM
SSD_GROUPS = 2
SSD_STATE = 128
SSD_CONV = 4
HEADS_PER_GROUP = SSD_HEADS // SSD_GROUPS
GROUP_WIDTH = SSD_INNER // SSD_GROUPS
BC_WIDTH = 2 * SSD_GROUPS * SSD_STATE
CONV_DIM = SSD_INNER + BC_WIDTH
SSD_CHUNK = 128
SSD_SUBCHUNKS = 2
CONV_HALO = 16

SB_HEADS = 16
SB_HEAD_DIM = 64
SB_INNER = SB_HEADS * SB_HEAD_DIM
SB_BLOCK = 128
SB_QT = 256
SB_GROUP = 4
SB_NG = 2
SB_LEAD = 2
SB_SATURATED = 105.0
SB_SOFTPLUS_LINEAR = 40.0

COL_ZSSD = 0
COL_XS = COL_ZSSD + SSD_INNER
COL_BC = COL_XS + SSD_INNER
COL_Q = COL_BC + BC_WIDTH
COL_K = COL_Q + SB_INNER
COL_V = COL_K + SB_INNER
COL_ZSB = COL_V + SB_INNER
COL_GATE = COL_ZSB + SB_INNER
PROJ_COLS = COL_GATE + 2 * D_MODEL
_SRC_DT = COL_Q
PROJ_CHUNK = 512

VMEM_LIMIT = 56 * 1024 * 1024
INPROJ_TM = 1024


def _softplus(x):
    return jnp.maximum(x, 0.0) + jnp.log1p(jnp.exp(-jnp.abs(x)))


def _sigmoid(x):
    return 1.0 / (1.0 + jnp.exp(-x))


def _silu(x):
    half = 0.5 * x
    return half + half * jnp.tanh(half)


def _inproj_kernel(x_ref, nw_ref, w_ref, wdt_ref, proj_ref, dt_ref, h_ref):
    @pl.when(pl.program_id(1) == 0)
    def _():
        x = x_ref[...]
        ms = jnp.mean(x * x, axis=-1, keepdims=True)
        h = (x * lax.rsqrt(ms + EPS) * nw_ref[...]).astype(BF16)
        h_ref[...] = h
        dt_ref[...] = jnp.dot(h, wdt_ref[...], preferred_element_type=F32)

    tn = PROJ_CHUNK
    w = w_ref[:, pl.ds(pl.multiple_of(pl.program_id(1) * tn, tn), tn)]
    proj_ref[...] = jnp.dot(h_ref[...], w, preferred_element_type=F32).astype(BF16)


def _inproj(x2, norm_pre, w_r, w_dt, tm):
    m = x2.shape[0]
    tn = PROJ_CHUNK
    return pl.pallas_call(
        _inproj_kernel,
        grid=(m // tm, PROJ_COLS // tn),
        in_specs=[
            pl.BlockSpec((tm, D_MODEL), lambda i, j: (i, 0)),
            pl.BlockSpec((1, D_MODEL), lambda i, j: (0, 0)),
            pl.BlockSpec((D_MODEL, PROJ_COLS), lambda i, j: (0, 0)),
            pl.BlockSpec((D_MODEL, LANES), lambda i, j: (0, 0)),
        ],
        out_specs=[
            pl.BlockSpec((None, tm, tn), lambda i, j: (j, i, 0)),
            pl.BlockSpec((tm, LANES), lambda i, j: (i, 0)),
        ],
        out_shape=[
            jax.ShapeDtypeStruct((PROJ_COLS // tn, m, tn), BF16),
            jax.ShapeDtypeStruct((m, LANES), F32),
        ],
        scratch_shapes=[pltpu.VMEM((tm, D_MODEL), BF16)],
        compiler_params=pltpu.CompilerParams(
            dimension_semantics=("parallel", "arbitrary"), vmem_limit_bytes=VMEM_LIMIT),
        name="inproj",
    )(x2, norm_pre, w_r, w_dt)


def _split2(x):
    hi = x.astype(BF16)
    lo = (x - hi.astype(F32)).astype(BF16)
    return hi, lo


def _ssd_kernel(z0_ref, z1_ref, xs0_ref, xs1_ref, bc_ref, dt_ref, cw_ref, cb_ref, dtb_ref,
                alog_ref, dsk_ref, nw_ref, yn_ref, state_ref, xe_ref, y_ref, *, t, nsub):
    c = pl.program_id(1)

    @pl.when(c == 0)
    def _():
        state_ref[...] = jnp.zeros_like(state_ref)
        xe_ref[pl.ds(nsub * t, CONV_HALO), :] = jnp.zeros((CONV_HALO, CONV_DIM), BF16)

    xe_ref[pl.ds(0, CONV_HALO), :] = xe_ref[pl.ds(nsub * t, CONV_HALO), :]
    for k, ref in enumerate((xs0_ref, xs1_ref, bc_ref)):
        xe_ref[pl.ds(CONV_HALO, nsub * t), pl.ds(k * PROJ_CHUNK, PROJ_CHUNK)] = ref[...]
    for sub in range(nsub):
        _ssd_subchunk(sub, t, (z0_ref, z1_ref), dt_ref, cw_ref, cb_ref, dtb_ref, alog_ref,
                      dsk_ref, nw_ref, yn_ref, state_ref, xe_ref, y_ref)


def _ssd_subchunk(sub, t, z_refs, dt_ref, cw_ref, cb_ref, dtb_ref, alog_ref, dsk_ref, nw_ref,
                  yn_ref, state_ref, xe_ref, y_ref):
    rows = pl.ds(sub * t, t)
    y_ref = y_ref.at[sub]
    xe = xe_ref[pl.ds(sub * t, t + CONV_HALO), :]
    out_row = lax.broadcasted_iota(jnp.int32, (t, t + CONV_HALO), 0)
    in_row = lax.broadcasted_iota(jnp.int32, (t, t + CONV_HALO), 1)
    conv = cb_ref[...] + cw_ref[pl.ds(SSD_CONV - 1, 1), :] * xe[CONV_HALO:, :].astype(F32)
    for j in range(SSD_CONV - 1):
        lag = SSD_CONV - 1 - j
        shift = jnp.where(in_row == out_row + (CONV_HALO - lag), 1.0, 0.0).astype(BF16)
        conv = conv + cw_ref[pl.ds(j, 1), :] * jnp.dot(shift, xe, preferred_element_type=F32)
    xbc = _silu(conv)
    xs = xbc[:, :SSD_INNER]

    dt = _softplus(dt_ref[rows, :] + dtb_ref[...])
    a_dt = dt * (-jnp.exp(alog_ref[...]))
    ri = lax.broadcasted_iota(jnp.int32, (t, t), 0)
    ci = lax.broadcasted_iota(jnp.int32, (t, t), 1)
    causal = ci <= ri
    tri = jnp.where(causal, 1.0, 0.0).astype(BF16)
    hi, lo = _split2(a_dt)
    lo2 = (a_dt - hi.astype(F32) - lo.astype(F32)).astype(BF16)
    a_cs = jnp.dot(jnp.concatenate([tri, tri, tri], axis=1),
                   jnp.concatenate([hi, lo, lo2], axis=0), preferred_element_type=F32)
    a_cs_t = a_cs.T
    a_last = a_cs[t - 1:t, :]

    ek = lax.broadcasted_iota(jnp.int32, (2 * LANES, SSD_INNER), 0) % LANES
    ec = lax.broadcasted_iota(jnp.int32, (2 * LANES, SSD_INNER), 1)
    expand2 = jnp.where(ec // SSD_HEAD_DIM == ek, 1.0, 0.0).astype(BF16)

    def expand(v):
        return jnp.dot(jnp.concatenate(_split2(v), axis=1), expand2, preferred_element_type=F32)

    x_dt = xs * expand(dt)
    x_dt_b = x_dt.astype(BF16)
    x_ds_b = (x_dt * jnp.exp(expand(a_last - a_cs))).astype(BF16)
    ecs_e = jnp.exp(expand(a_cs))
    chunk_decay_e = expand(jnp.exp(jnp.broadcast_to(a_last, (SSD_STATE, LANES))))

    lane = lax.broadcasted_iota(jnp.int32, (t, LANES), 1)
    first_head = lane < SSD_HEAD_DIM
    for g in range(SSD_GROUPS):
        b_f = xbc[:, SSD_INNER + g * SSD_STATE:SSD_INNER + (g + 1) * SSD_STATE]
        b_g = b_f.astype(BF16)
        b_t = b_f.T.astype(BF16)
        c_g = xbc[:, SSD_INNER + (SSD_GROUPS + g) * SSD_STATE:
                  SSD_INNER + (SSD_GROUPS + g + 1) * SSD_STATE].astype(BF16)
        scores = lax.dot_general(c_g, b_g, (((1,), (1,)), ((), ())), preferred_element_type=F32)
        for pr in range(HEADS_PER_GROUP // 2):
            col0 = g * GROUP_WIDTH + pr * LANES
            x_pair = x_dt_b[:, col0:col0 + LANES]
            halves = []
            for hh in range(2):
                h = g * HEADS_PER_GROUP + pr * 2 + hh
                seg = a_cs[:, h:h + 1] - a_cs_t[h:h + 1, :]
                decay = jnp.exp(jnp.where(causal, seg, -jnp.inf))
                attn = (scores * decay).astype(BF16)
                halves.append(jnp.dot(attn, x_pair, preferred_element_type=F32))
            y_ref[:, pl.ds(col0, LANES)] = jnp.where(first_head, halves[0], halves[1])

        gs = pl.ds(g * GROUP_WIDTH, GROUP_WIDTH)
        cols = slice(g * GROUP_WIDTH, (g + 1) * GROUP_WIDTH)
        st = state_ref[g]
        y_off = jnp.dot(c_g, st.astype(BF16), preferred_element_type=F32)
        y_ref[:, gs] = y_ref[:, gs] + y_off * ecs_e[:, cols]
        upd = jnp.dot(b_t, x_ds_b[:, cols], preferred_element_type=F32)
        state_ref[g] = st * chunk_decay_e[:, cols] + upd

    y = y_ref[...] + xs * dsk_ref[...]
    z = jnp.concatenate([ref[rows, :] for ref in z_refs], axis=1).astype(F32)
    gated = y * _silu(z)
    for g in range(SSD_GROUPS):
        gg = gated[:, g * GROUP_WIDTH:(g + 1) * GROUP_WIDTH]
        ms = jnp.mean(gg * gg, axis=-1, keepdims=True)
        yn_ref[rows, pl.ds(g * GROUP_WIDTH, GROUP_WIDTH)] = (
            gg * lax.rsqrt(ms + EPS) * nw_ref[:, pl.ds(g * GROUP_WIDTH, GROUP_WIDTH)]).astype(BF16)


def _ssd(proj, dt_raw, conv_w, conv_b, dt_bias, a_log, d_skip_e, ssd_norm, bsz, seq, t, nsub):
    m = bsz * seq
    rows = t * nsub
    nc = seq // rows
    row = lambda b, c: b * nc + c
    const = lambda b, c: (0, 0)
    chunk = lambda col: pl.BlockSpec(
        (None, rows, PROJ_CHUNK), lambda b, c: (col // PROJ_CHUNK, row(b, c), 0))
    return pl.pallas_call(
        functools.partial(_ssd_kernel, t=t, nsub=nsub),
        grid=(bsz, nc),
        in_specs=[
            chunk(COL_ZSSD), chunk(COL_ZSSD + PROJ_CHUNK),
            chunk(COL_XS), chunk(COL_XS + PROJ_CHUNK), chunk(COL_BC),
            pl.BlockSpec((rows, LANES), lambda b, c: (row(b, c), 0)),
            pl.BlockSpec((SSD_CONV, CONV_DIM), const),
            pl.BlockSpec((1, CONV_DIM), const),
            pl.BlockSpec((1, LANES), const),
            pl.BlockSpec((1, LANES), const),
            pl.BlockSpec((1, SSD_INNER), const),
            pl.BlockSpec((1, SSD_INNER), const),
        ],
        out_specs=pl.BlockSpec((rows, SSD_INNER), lambda b, c: (row(b, c), 0)),
        out_shape=jax.ShapeDtypeStruct((m, SSD_INNER), BF16),
        scratch_shapes=[
            pltpu.VMEM((SSD_GROUPS, SSD_STATE, GROUP_WIDTH), F32),
            pltpu.VMEM((rows + CONV_HALO, CONV_DIM), BF16),
            pltpu.VMEM((nsub, t, SSD_INNER), F32),
        ],
        compiler_params=pltpu.CompilerParams(
            dimension_semantics=("parallel", "arbitrary"), vmem_limit_bytes=VMEM_LIMIT),
        name="ssd",
    )(proj, proj, proj, proj, proj, dt_raw, conv_w, conv_b, dt_bias, a_log, d_skip_e, ssd_norm)


def _suffix_sum_matrix():
    j = np.arange(2 * SB_BLOCK)[:, None]
    n = np.arange(2 * SB_BLOCK)[None, :]
    return (j > n).astype(np.float32)


def _sb_kernel(q_ref, k_ref, v_ref, zsb_ref, uu_ref, o_ref, qm_ref, acc_ref, car_ref, flag_ref,
               *, qt, ng):
    tile = pl.program_id(2)
    blk = SB_BLOCK
    gw = SB_GROUP * SB_HEAD_DIM
    q_head = lax.broadcasted_iota(jnp.int32, (qt, gw), 1) // SB_HEAD_DIM
    for g in range(ng):
        q = q_ref[:, pl.ds(g * gw, gw)] * jnp.asarray(SB_HEAD_DIM ** -0.5, BF16)
        for h in range(SB_GROUP):
            qm_ref[g * SB_GROUP + h] = jnp.where(q_head == h, q, jnp.zeros_like(q))
    acc_ref[...] = jnp.zeros_like(acc_ref)
    car_ref[...] = jnp.zeros_like(car_ref)
    flag_ref[0] = jnp.int32(0)
    uu = uu_ref[...]

    def keep_mask(spec):
        j0, _, r0, nrows, diag_block = spec
        qi = tile * qt + r0 + lax.broadcasted_iota(jnp.int32, (nrows, blk), 0)
        ki = (j0 + diag_block) * blk + lax.broadcasted_iota(jnp.int32, (nrows, blk), 1)
        return ki < qi

    def mask_diagonal(spec, x):
        nblk, diag_block = spec[1], spec[4]
        if diag_block is None:
            return x
        parts = [x[:, c * blk:(c + 1) * blk] for c in range(nblk)]
        parts[diag_block] = jnp.where(keep_mask(spec), parts[diag_block], 0.0)
        return jnp.concatenate(parts, axis=1) if nblk > 1 else parts[0]

    def key_rows(spec):
        j0, nblk = spec[0], spec[1]
        return pl.ds(pl.multiple_of(j0 * blk, blk), nblk * blk)

    def logits_stage(spec, g, h):
        k2 = k_ref[key_rows(spec), pl.ds(g * gw, gw)]
        return lax.dot_general(
            qm_ref[g * SB_GROUP + h, pl.ds(spec[2], spec[3]), :], k2, (((1,), (1,)), ((), ())),
            preferred_element_type=F32)

    def sums_stage(spec, s):
        span = spec[1] * blk
        sp = jnp.where(s > SB_SOFTPLUS_LINEAR, s, jnp.log(1.0 + jnp.exp(s)))
        terms = mask_diagonal(spec, sp).astype(BF16)
        suffix = jnp.dot(terms, uu[:span, :span], preferred_element_type=F32)
        return s - sp, suffix, suffix[:, :1] + terms[:, :1].astype(F32)

    def weights_stage(spec, g, h, log_beta, suffix, total):
        rows = pl.ds(spec[2], spec[3])
        hh = g * SB_GROUP + h
        car = car_ref[hh, rows, :]
        car_ref[hh, rows, :] = car + total
        return mask_diagonal(spec, jnp.exp(log_beta - suffix - car)).astype(BF16)

    def output_stage(spec, g, ws):
        rows = pl.ds(spec[2], spec[3])
        v_head = lax.broadcasted_iota(jnp.int32, (spec[1] * blk, gw), 1) // SB_HEAD_DIM
        v2 = v_ref[key_rows(spec), pl.ds(g * gw, gw)]
        vm = jnp.concatenate(
            [jnp.where(v_head == h, v2, jnp.zeros_like(v2)) for h in range(SB_GROUP)], axis=0)
        pv = jnp.dot(jnp.concatenate(ws, axis=1), vm, preferred_element_type=F32)
        acc_ref[rows, pl.ds(g * gw, gw)] = acc_ref[rows, pl.ds(g * gw, gw)] + pv

    def run(specs):
        items = [(spec, g, h) for spec in specs for g in range(ng) for h in range(SB_GROUP)]
        logits, mids, ws = {}, {}, []
        for i in range(len(items) + 2 * SB_LEAD):
            if i < len(items):
                logits[i] = logits_stage(*items[i])
            j = i - SB_LEAD
            if 0 <= j < len(items):
                mids[j] = sums_stage(items[j][0], logits.pop(j))
            j = i - 2 * SB_LEAD
            if 0 <= j < len(items):
                spec, g, h = items[j]
                ws.append(weights_stage(spec, g, h, *mids.pop(j)))
                if h == SB_GROUP - 1:
                    output_stage(spec, g, ws)
                    ws = []

    nsub = qt // blk
    diagonal = []
    for c in reversed(range(0, nsub, 2)):
        diagonal.append((tile * nsub + c, 2, (c + 1) * blk, qt - (c + 1) * blk, 1))
        diagonal.append((tile * nsub + c, 1, c * blk, blk, 0))

    def left_spec(it):
        return (tile * nsub - 2 * (it + 1), 2, 0, qt, None)

    def flag_saturation():
        flag_ref[0] = (jnp.min(car_ref[...]) >= SB_SATURATED).astype(jnp.int32)

    @pl.when(tile == 0)
    def _():
        run(diagonal)

    assert nsub == 2
    first = tile * nsub

    @pl.when(tile > 0)
    def _():
        run(diagonal + [(first - 1, 1, 0, qt, None), (first - 2, 1, 0, blk, None)])
        flag_saturation()

    @pl.when(jnp.logical_and(tile > 0, flag_ref[0] == 0))
    def _():
        run([(first - 2, 1, blk, qt - blk, None)])
        flag_saturation()

    n_steps = tile * (nsub // 2)

    def more(it):
        return jnp.logical_and(it < n_steps, flag_ref[0] == 0)

    def body(it):
        run([left_spec(it)])
        flag_saturation()
        return it + 1

    lax.while_loop(more, body, jnp.int32(1))
    z = zsb_ref[...].astype(F32)
    o_ref[...] = (acc_ref[...] * (z * _sigmoid(z))).astype(BF16)


def _stick_breaking(proj, uu, bsz, seq, qt, ng):
    m = bsz * seq
    nq = seq // qt
    width = ng * SB_GROUP * SB_HEAD_DIM
    nh = ng * SB_GROUP
    assert width == PROJ_CHUNK
    return pl.pallas_call(
        functools.partial(_sb_kernel, qt=qt, ng=ng),
        grid=(bsz, SB_HEADS // nh, nq),
        in_specs=[
            pl.BlockSpec((None, qt, width), lambda b, p, i: (COL_Q // width + p, b * nq + i, 0)),
            pl.BlockSpec((None, seq, width), lambda b, p, i: (COL_K // width + p, b, 0)),
            pl.BlockSpec((None, seq, width), lambda b, p, i: (COL_V // width + p, b, 0)),
            pl.BlockSpec((None, qt, width), lambda b, p, i: (COL_ZSB // width + p, b * nq + i, 0)),
            pl.BlockSpec((2 * SB_BLOCK, 2 * SB_BLOCK), lambda b, p, i: (0, 0)),
        ],
        out_specs=pl.BlockSpec((qt, width), lambda b, p, i: (b * nq + i, p)),
        out_shape=jax.ShapeDtypeStruct((m, SB_INNER), BF16),
        scratch_shapes=[
            pltpu.VMEM((nh, qt, SB_GROUP * SB_HEAD_DIM), BF16),
            pltpu.VMEM((qt, width), F32),
            pltpu.VMEM((nh, qt, 1), F32),
            pltpu.SMEM((1,), jnp.int32),
        ],
        compiler_params=pltpu.CompilerParams(
            dimension_semantics=("parallel", "parallel", "arbitrary"),
            vmem_limit_bytes=VMEM_LIMIT),
        name="stick_breaking",
    )(proj, proj, proj, proj, uu)


def _merge_kernel(yn_ref, osb_ref, g0_ref, g1_ref, g2_ref, g3_ref, x_ref, bg_ref, wssd_ref,
                  wsb_ref, wout_ref, nw_ref, out_ref):
    y_ssd = jnp.dot(yn_ref[...], wssd_ref[...], preferred_element_type=F32)
    y_sb = jnp.dot(osb_ref[...], wsb_ref[...], preferred_element_type=F32)
    gates = jnp.concatenate([g0_ref[...], g1_ref[...], g2_ref[...], g3_ref[...]], axis=1)
    g = _sigmoid(gates.astype(F32) + bg_ref[...])
    merged = (g[:, :D_MODEL] * y_ssd + g[:, D_MODEL:] * y_sb).astype(BF16)
    out = jnp.dot(merged, wout_ref[...], preferred_element_type=F32)
    ms = jnp.mean(out * out, axis=-1, keepdims=True)
    out_ref[...] = x_ref[...] + out * lax.rsqrt(ms + EPS) * nw_ref[...]


def _merge(yn, osb, proj, x2, b_gate, w_ssd, w_sb, w_out, norm_post, tm):
    m = x2.shape[0]
    rows = lambda i: (i, 0)
    const = lambda i: (0, 0)
    gate = lambda k: pl.BlockSpec(
        (None, tm, PROJ_CHUNK), lambda i: (COL_GATE // PROJ_CHUNK + k, i, 0))
    return pl.pallas_call(
        _merge_kernel,
        grid=(m // tm,),
        in_specs=[
            pl.BlockSpec((tm, D_MODEL), rows),
            pl.BlockSpec((tm, D_MODEL), rows),
            gate(0), gate(1), gate(2), gate(3),
            pl.BlockSpec((tm, D_MODEL), rows),
            pl.BlockSpec((1, 2 * D_MODEL), const),
            pl.BlockSpec((D_MODEL, D_MODEL), const),
            pl.BlockSpec((D_MODEL, D_MODEL), const),
            pl.BlockSpec((D_MODEL, D_MODEL), const),
            pl.BlockSpec((1, D_MODEL), const),
        ],
        out_specs=pl.BlockSpec((tm, D_MODEL), rows),
        out_shape=jax.ShapeDtypeStruct((m, D_MODEL), F32),
        compiler_params=pltpu.CompilerParams(
            dimension_semantics=("parallel",), vmem_limit_bytes=VMEM_LIMIT),
        name="merge",
    )(yn, osb, proj, proj, proj, proj, x2, b_gate, w_ssd, w_sb, w_out, norm_post)


def _layer(x, norm_pre, w_in, b_gate, conv_w, conv_b, dt_bias, a_log, d_skip, ssd_norm,
           w_ssd_proj, w_sb_proj, w_out, norm_post):
    bsz, seq, _ = x.shape
    m = bsz * seq
    x2 = x.reshape(m, D_MODEL)

    w_r = jnp.concatenate([w_in[:, :_SRC_DT], w_in[:, _SRC_DT + SSD_HEADS:]], axis=1).astype(BF16)
    w_dt = jnp.pad(w_in[:, _SRC_DT:_SRC_DT + SSD_HEADS],
                   ((0, 0), (0, LANES - SSD_HEADS))).astype(BF16)
    pad_heads = lambda v: jnp.pad(v, (0, LANES - SSD_HEADS)).reshape(1, LANES)

    tm = min(512, m)
    proj, dt_raw = _inproj(x2, norm_pre.reshape(1, D_MODEL), w_r, w_dt, min(INPROJ_TM, m))

    yn = _ssd(proj, dt_raw, conv_w, conv_b.reshape(1, CONV_DIM), pad_heads(dt_bias),
              pad_heads(a_log), jnp.repeat(d_skip, SSD_HEAD_DIM).reshape(1, SSD_INNER),
              ssd_norm.reshape(1, SSD_INNER), bsz, seq, SSD_CHUNK,
              min(SSD_SUBCHUNKS, seq // SSD_CHUNK))

    osb = _stick_breaking(proj, jnp.asarray(_suffix_sum_matrix(), BF16), bsz, seq,
                          min(SB_QT, seq), SB_NG)

    out = _merge(yn, osb, proj, x2, b_gate.reshape(1, 2 * D_MODEL), w_ssd_proj.astype(BF16),
                 w_sb_proj.astype(BF16), w_out.astype(BF16), norm_post.reshape(1, D_MODEL), tm)
    return out.reshape(bsz, seq, D_MODEL)


def kernel(x, norm_pre, w_in, b_gate, conv_w, conv_b, dt_bias, a_log, d_skip, ssd_norm,
           w_ssd_proj, w_sb_proj, w_out, norm_post):
    for layer in range(norm_pre.shape[0]):
        x = _layer(x, norm_pre[layer], w_in[layer], b_gate[layer], conv_w[layer],
                   conv_b[layer], dt_bias[layer], a_log[layer], d_skip[layer],
                   ssd_norm[layer], w_ssd_proj[layer], w_sb_proj[layer], w_out[layer],
                   norm_post[layer])
    return x
```

```python
import functools

import numpy as np
import jax
import jax.numpy as jnp
from jax import lax
from jax.experimental import pallas as pl
from jax.experimental.pallas import tpu as pltpu

F32 = jnp.float32
BF16 = jnp.bfloat16
HIGHEST = lax.Precision.HIGHEST

D_MODEL = 1024
EPS = 1e-6
LANES = 128

SSD_HEADS = 16
SSD_HEAD_DIM = 64
SSD_INNER = SSD_HEADS * SSD_HEAD_DIM
SSD_GROUPS = 2
SSD_STATE = 128
SSD_CONV = 4
HEADS_PER_GROUP = SSD_HEADS // SSD_GROUPS
GROUP_WIDTH = SSD_INNER // SSD_GROUPS
BC_WIDTH = 2 * SSD_GROUPS * SSD_STATE
CONV_DIM = SSD_INNER + BC_WIDTH
SSD_CHUNK = 128
SSD_SUBCHUNKS = 2
CONV_HALO = 16

SB_HEADS = 16
SB_HEAD_DIM = 64
SB_INNER = SB_HEADS * SB_HEAD_DIM
SB_BLOCK = 128
SB_QT = 256
SB_GROUP = 4
SB_NG = 2
SB_LEAD = 2
SB_SATURATED = 105.0
SB_SOFTPLUS_LINEAR = 40.0

COL_ZSSD = 0
COL_XS = COL_ZSSD + SSD_INNER
COL_BC = COL_XS + SSD_INNER
COL_Q = COL_BC + BC_WIDTH
COL_K = COL_Q + SB_INNER
COL_V = COL_K + SB_INNER
COL_ZSB = COL_V + SB_INNER
COL_GATE = COL_ZSB + SB_INNER
PROJ_COLS = COL_GATE + 2 * D_MODEL
_SRC_DT = COL_Q
PROJ_CHUNK = 512

VMEM_LIMIT = 56 * 1024 * 1024
INPROJ_TM = 2048
MERGE_TM = 1024


def _softplus(x):
    return jnp.maximum(x, 0.0) + jnp.log1p(jnp.exp(-jnp.abs(x)))


def _sigmoid(x):
    return 1.0 / (1.0 + jnp.exp(-x))


def _silu(x):
    half = 0.5 * x
    return half + half * jnp.tanh(half)


def _inproj_kernel(x_ref, nw_ref, w_ref, wdt_ref, proj_ref, dt_ref, h_ref):
    @pl.when(pl.program_id(1) == 0)
    def _():
        x = x_ref[...]
        ms = jnp.mean(x * x, axis=-1, keepdims=True)
        h = (x * lax.rsqrt(ms + EPS) * nw_ref[...]).astype(BF16)
        h_ref[...] = h
        dt_ref[...] = jnp.dot(h, wdt_ref[...], preferred_element_type=F32)

    tn = PROJ_CHUNK
    w = w_ref[:, pl.ds(pl.multiple_of(pl.program_id(1) * tn, tn), tn)]
    proj_ref[...] = jnp.dot(h_ref[...], w, preferred_element_type=F32).astype(BF16)


def _inproj(x2, norm_pre, w_r, w_dt, tm):
    m = x2.shape[0]
    tn = PROJ_CHUNK
    return pl.pallas_call(
        _inproj_kernel,
        grid=(m // tm, PROJ_COLS // tn),
        in_specs=[
            pl.BlockSpec((tm, D_MODEL), lambda i, j: (i, 0)),
            pl.BlockSpec((1, D_MODEL), lambda i, j: (0, 0)),
            pl.BlockSpec((D_MODEL, PROJ_COLS), lambda i, j: (0, 0), pipeline_mode=pl.Buffered(1)),
            pl.BlockSpec((D_MODEL, LANES), lambda i, j: (0, 0), pipeline_mode=pl.Buffered(1)),
        ],
        out_specs=[
            pl.BlockSpec((None, tm, tn), lambda i, j: (j, i, 0)),
            pl.BlockSpec((tm, LANES), lambda i, j: (i, 0)),
        ],
        out_shape=[
            jax.ShapeDtypeStruct((PROJ_COLS // tn, m, tn), BF16),
            jax.ShapeDtypeStruct((m, LANES), F32),
        ],
        scratch_shapes=[pltpu.VMEM((tm, D_MODEL), BF16)],
        compiler_params=pltpu.CompilerParams(
            dimension_semantics=("parallel", "arbitrary"), vmem_limit_bytes=VMEM_LIMIT),
        name="inproj",
    )(x2, norm_pre, w_r, w_dt)


def _split2(x):
    hi = x.astype(BF16)
    lo = (x - hi.astype(F32)).astype(BF16)
    return hi, lo


def _ssd_kernel(z0_ref, z1_ref, xs0_ref, xs1_ref, bc_ref, dt_ref, cw_ref, cb_ref, dtb_ref,
                alog_ref, dsk_ref, nw_ref, yn_ref, state_ref, xe_ref, y_ref, *, t, nsub):
    c = pl.program_id(1)

    @pl.when(c == 0)
    def _():
        state_ref[...] = jnp.zeros_like(state_ref)
        xe_ref[pl.ds(nsub * t, CONV_HALO), :] = jnp.zeros((CONV_HALO, CONV_DIM), BF16)

    xe_ref[pl.ds(0, CONV_HALO), :] = xe_ref[pl.ds(nsub * t, CONV_HALO), :]
    for k, ref in enumerate((xs0_ref, xs1_ref, bc_ref)):
        xe_ref[pl.ds(CONV_HALO, nsub * t), pl.ds(k * PROJ_CHUNK, PROJ_CHUNK)] = ref[...]
    for sub in range(nsub):
        _ssd_subchunk(sub, t, (z0_ref, z1_ref), dt_ref, cw_ref, cb_ref, dtb_ref, alog_ref,
                      dsk_ref, nw_ref, yn_ref, state_ref, xe_ref, y_ref)


def _ssd_subchunk(sub, t, z_refs, dt_ref, cw_ref, cb_ref, dtb_ref, alog_ref, dsk_ref, nw_ref,
                  yn_ref, state_ref, xe_ref, y_ref):
    rows = pl.ds(sub * t, t)
    y_ref = y_ref.at[sub]
    xe = xe_ref[pl.ds(sub * t, t + CONV_HALO), :]
    out_row = lax.broadcasted_iota(jnp.int32, (t, t + CONV_HALO), 0)
    in_row = lax.broadcasted_iota(jnp.int32, (t, t + CONV_HALO), 1)
    conv = cb_ref[...] + cw_ref[pl.ds(SSD_CONV - 1, 1), :] * xe[CONV_HALO:, :].astype(F32)
    for j in range(SSD_CONV - 1):
        lag = SSD_CONV - 1 - j
        shift = jnp.where(in_row == out_row + (CONV_HALO - lag), 1.0, 0.0).astype(BF16)
        conv = conv + cw_ref[pl.ds(j, 1), :] * jnp.dot(shift, xe, preferred_element_type=F32)
    xbc = _silu(conv)
    xs = xbc[:, :SSD_INNER]

    dt = _softplus(dt_ref[rows, :] + dtb_ref[...])
    a_dt = dt * (-jnp.exp(alog_ref[...]))
    ri = lax.broadcasted_iota(jnp.int32, (t, t), 0)
    ci = lax.broadcasted_iota(jnp.int32, (t, t), 1)
    causal = ci <= ri
    tri = jnp.where(causal, 1.0, 0.0).astype(BF16)
    hi, lo = _split2(a_dt)
    lo2 = (a_dt - hi.astype(F32) - lo.astype(F32)).astype(BF16)
    a_cs = jnp.dot(jnp.concatenate([tri, tri, tri], axis=1),
                   jnp.concatenate([hi, lo, lo2], axis=0), preferred_element_type=F32)
    a_cs_t = a_cs.T
    a_last = a_cs[t - 1:t, :]

    ek = lax.broadcasted_iota(jnp.int32, (2 * LANES, SSD_INNER), 0) % LANES
    ec = lax.broadcasted_iota(jnp.int32, (2 * LANES, SSD_INNER), 1)
    expand2 = jnp.where(ec // SSD_HEAD_DIM == ek, 1.0, 0.0).astype(BF16)

    def expand(v):
        return jnp.dot(jnp.concatenate(_split2(v), axis=1), expand2, preferred_element_type=F32)

    x_dt = xs * expand(dt)
    x_dt_b = x_dt.astype(BF16)
    x_ds_b = (x_dt * jnp.exp(expand(a_last - a_cs))).astype(BF16)
    ecs_e = jnp.exp(expand(a_cs))
    chunk_decay_e = expand(jnp.exp(jnp.broadcast_to(a_last, (SSD_STATE, LANES))))

    lane = lax.broadcasted_iota(jnp.int32, (t, LANES), 1)
    first_head = lane < SSD_HEAD_DIM
    for g in range(SSD_GROUPS):
        b_f = xbc[:, SSD_INNER + g * SSD_STATE:SSD_INNER + (g + 1) * SSD_STATE]
        b_g = b_f.astype(BF16)
        b_t = b_f.T.astype(BF16)
        c_g = xbc[:, SSD_INNER + (SSD_GROUPS + g) * SSD_STATE:
                  SSD_INNER + (SSD_GROUPS + g + 1) * SSD_STATE].astype(BF16)
        scores = lax.dot_general(c_g, b_g, (((1,), (1,)), ((), ())), preferred_element_type=F32)
        for pr in range(HEADS_PER_GROUP // 2):
            col0 = g * GROUP_WIDTH + pr * LANES
            x_pair = x_dt_b[:, col0:col0 + LANES]
            halves = []
            for hh in range(2):
                h = g * HEADS_PER_GROUP + pr * 2 + hh
                seg = a_cs[:, h:h + 1] - a_cs_t[h:h + 1, :]
                decay = jnp.exp(jnp.where(causal, seg, -jnp.inf))
                attn = (scores * decay).astype(BF16)
                halves.append(jnp.dot(attn, x_pair, preferred_element_type=F32))
            y_ref[:, pl.ds(col0, LANES)] = jnp.where(first_head, halves[0], halves[1])

        gs = pl.ds(g * GROUP_WIDTH, GROUP_WIDTH)
        cols = slice(g * GROUP_WIDTH, (g + 1) * GROUP_WIDTH)
        st = state_ref[g]
        y_off = jnp.dot(c_g, st.astype(BF16), preferred_element_type=F32)
        y_ref[:, gs] = y_ref[:, gs] + y_off * ecs_e[:, cols]
        upd = jnp.dot(b_t, x_ds_b[:, cols], preferred_element_type=F32)
        state_ref[g] = st * chunk_decay_e[:, cols] + upd

    y = y_ref[...] + xs * dsk_ref[...]
    z = jnp.concatenate([ref[rows, :] for ref in z_refs], axis=1).astype(F32)
    gated = y * _silu(z)
    for g in range(SSD_GROUPS):
        gg = gated[:, g * GROUP_WIDTH:(g + 1) * GROUP_WIDTH]
        ms = jnp.mean(gg * gg, axis=-1, keepdims=True)
        yn_ref[rows, pl.ds(g * GROUP_WIDTH, GROUP_WIDTH)] = (
            gg * lax.rsqrt(ms + EPS) * nw_ref[:, pl.ds(g * GROUP_WIDTH, GROUP_WIDTH)]).astype(BF16)


def _ssd(proj, dt_raw, conv_w, conv_b, dt_bias, a_log, d_skip_e, ssd_norm, bsz, seq, t, nsub):
    m = bsz * seq
    rows = t * nsub
    nc = seq // rows
    row = lambda b, c: b * nc + c
    const = lambda b, c: (0, 0)
    chunk = lambda col: pl.BlockSpec(
        (None, rows, PROJ_CHUNK), lambda b, c: (col // PROJ_CHUNK, row(b, c), 0))
    return pl.pallas_call(
        functools.partial(_ssd_kernel, t=t, nsub=nsub),
        grid=(bsz, nc),
        in_specs=[
            chunk(COL_ZSSD), chunk(COL_ZSSD + PROJ_CHUNK),
            chunk(COL_XS), chunk(COL_XS + PROJ_CHUNK), chunk(COL_BC),
            pl.BlockSpec((rows, LANES), lambda b, c: (row(b, c), 0)),
            pl.BlockSpec((SSD_CONV, CONV_DIM), const),
            pl.BlockSpec((1, CONV_DIM), const),
            pl.BlockSpec((1, LANES), const),
            pl.BlockSpec((1, LANES), const),
            pl.BlockSpec((1, SSD_INNER), const),
            pl.BlockSpec((1, SSD_INNER), const),
        ],
        out_specs=pl.BlockSpec((rows, SSD_INNER), lambda b, c: (row(b, c), 0)),
        out_shape=jax.ShapeDtypeStruct((m, SSD_INNER), BF16),
        scratch_shapes=[
            pltpu.VMEM((SSD_GROUPS, SSD_STATE, GROUP_WIDTH), F32),
            pltpu.VMEM((rows + CONV_HALO, CONV_DIM), BF16),
            pltpu.VMEM((nsub, t, SSD_INNER), F32),
        ],
        compiler_params=pltpu.CompilerParams(
            dimension_semantics=("parallel", "arbitrary"), vmem_limit_bytes=VMEM_LIMIT),
        name="ssd",
    )(proj, proj, proj, proj, proj, dt_raw, conv_w, conv_b, dt_bias, a_log, d_skip_e, ssd_norm)


def _suffix_sum_matrix():
    j = np.arange(2 * SB_BLOCK)[:, None]
    n = np.arange(2 * SB_BLOCK)[None, :]
    return (j > n).astype(np.float32)


def _sb_kernel(q_ref, k_ref, v_ref, zsb_ref, uu_ref, o_ref, qm_ref, acc_ref, car_ref, flag_ref,
               *, qt, ng):
    tile = pl.program_id(2)
    blk = SB_BLOCK
    gw = SB_GROUP * SB_HEAD_DIM
    q_head = lax.broadcasted_iota(jnp.int32, (qt, gw), 1) // SB_HEAD_DIM
    for g in range(ng):
        q = q_ref[:, pl.ds(g * gw, gw)] * jnp.asarray(SB_HEAD_DIM ** -0.5, BF16)
        for h in range(SB_GROUP):
            qm_ref[g * SB_GROUP + h] = jnp.where(q_head == h, q, jnp.zeros_like(q))
    acc_ref[...] = jnp.zeros_like(acc_ref)
    car_ref[...] = jnp.zeros_like(car_ref)
    flag_ref[0] = jnp.int32(0)
    uu = uu_ref[...]

    def keep_mask(spec):
        j0, _, r0, nrows, diag_block = spec
        qi = tile * qt + r0 + lax.broadcasted_iota(jnp.int32, (nrows, blk), 0)
        ki = (j0 + diag_block) * blk + lax.broadcasted_iota(jnp.int32, (nrows, blk), 1)
        return ki < qi

    def mask_diagonal(spec, x):
        nblk, diag_block = spec[1], spec[4]
        if diag_block is None:
            return x
        parts = [x[:, c * blk:(c + 1) * blk] for c in range(nblk)]
        parts[diag_block] = jnp.where(keep_mask(spec), parts[diag_block], 0.0)
        return jnp.concatenate(parts, axis=1) if nblk > 1 else parts[0]

    def key_rows(spec):
        j0, nblk = spec[0], spec[1]
        return pl.ds(pl.multiple_of(j0 * blk, blk), nblk * blk)

    def logits_stage(spec, g, h):
        k2 = k_ref[key_rows(spec), pl.ds(g * gw, gw)]
        return lax.dot_general(
            qm_ref[g * SB_GROUP + h, pl.ds(spec[2], spec[3]), :], k2, (((1,), (1,)), ((), ())),
            preferred_element_type=F32)

    def sums_stage(spec, s):
        span = spec[1] * blk
        sp = jnp.where(s > SB_SOFTPLUS_LINEAR, s, jnp.log(1.0 + jnp.exp(s)))
        terms = mask_diagonal(spec, sp).astype(BF16)
        suffix = jnp.dot(terms, uu[:span, :span], preferred_element_type=F32)
        return s - sp, suffix, suffix[:, :1] + terms[:, :1].astype(F32)

    def weights_stage(spec, g, h, log_beta, suffix, total):
        rows = pl.ds(spec[2], spec[3])
        hh = g * SB_GROUP + h
        car = car_ref[hh, rows, :]
        car_ref[hh, rows, :] = car + total
        return mask_diagonal(spec, jnp.exp(log_beta - suffix - car)).astype(BF16)

    def output_stage(spec, g, ws):
        rows = pl.ds(spec[2], spec[3])
        v_head = lax.broadcasted_iota(jnp.int32, (spec[1] * blk, gw), 1) // SB_HEAD_DIM
        v2 = v_ref[key_rows(spec), pl.ds(g * gw, gw)]
        vm = jnp.concatenate(
            [jnp.where(v_head == h, v2, jnp.zeros_like(v2)) for h in range(SB_GROUP)], axis=0)
        pv = jnp.dot(jnp.concatenate(ws, axis=1), vm, preferred_element_type=F32)
        acc_ref[rows, pl.ds(g * gw, gw)] = acc_ref[rows, pl.ds(g * gw, gw)] + pv

    def run(specs):
        items = [(spec, g, h) for spec in specs for g in range(ng) for h in range(SB_GROUP)]
        logits, mids, ws = {}, {}, []
        for i in range(len(items) + 2 * SB_LEAD):
            if i < len(items):
                logits[i] = logits_stage(*items[i])
            j = i - SB_LEAD
            if 0 <= j < len(items):
                mids[j] = sums_stage(items[j][0], logits.pop(j))
            j = i - 2 * SB_LEAD
            if 0 <= j < len(items):
                spec, g, h = items[j]
                ws.append(weights_stage(spec, g, h, *mids.pop(j)))
                if h == SB_GROUP - 1:
                    output_stage(spec, g, ws)
                    ws = []

    nsub = qt // blk
    diagonal = []
    for c in reversed(range(0, nsub, 2)):
        diagonal.append((tile * nsub + c, 2, (c + 1) * blk, qt - (c + 1) * blk, 1))
        diagonal.append((tile * nsub + c, 1, c * blk, blk, 0))

    def left_spec(it):
        return (tile * nsub - 2 * (it + 1), 2, 0, qt, None)

    def flag_saturation():
        flag_ref[0] = (jnp.min(car_ref[...]) >= SB_SATURATED).astype(jnp.int32)

    @pl.when(tile == 0)
    def _():
        run(diagonal)

    assert nsub == 2
    first = tile * nsub

    @pl.when(tile > 0)
    def _():
        run(diagonal + [(first - 1, 1, 0, qt, None), (first - 2, 1, 0, blk, None)])
        flag_saturation()

    @pl.when(jnp.logical_and(tile > 0, flag_ref[0] == 0))
    def _():
        run([(first - 2, 1, blk, qt - blk, None)])
        flag_saturation()

    n_steps = tile * (nsub // 2)

    def more(it):
        return jnp.logical_and(it < n_steps, flag_ref[0] == 0)

    def body(it):
        run([left_spec(it)])
        flag_saturation()
        return it + 1

    lax.while_loop(more, body, jnp.int32(1))
    z = zsb_ref[...].astype(F32)
    o_ref[...] = (acc_ref[...] * (z * _sigmoid(z))).astype(BF16)


def _stick_breaking(proj, uu, bsz, seq, qt, ng):
    m = bsz * seq
    nq = seq // qt
    width = ng * SB_GROUP * SB_HEAD_DIM
    nh = ng * SB_GROUP
    assert width == PROJ_CHUNK
    return pl.pallas_call(
        functools.partial(_sb_kernel, qt=qt, ng=ng),
        grid=(bsz, SB_HEADS // nh, nq),
        in_specs=[
            pl.BlockSpec((None, qt, width), lambda b, p, i: (COL_Q // width + p, b * nq + i, 0)),
            pl.BlockSpec((None, seq, width), lambda b, p, i: (COL_K // width + p, b, 0)),
            pl.BlockSpec((None, seq, width), lambda b, p, i: (COL_V // width + p, b, 0)),
            pl.BlockSpec((None, qt, width), lambda b, p, i: (COL_ZSB // width + p, b * nq + i, 0)),
            pl.BlockSpec((2 * SB_BLOCK, 2 * SB_BLOCK), lambda b, p, i: (0, 0)),
        ],
        out_specs=pl.BlockSpec((qt, width), lambda b, p, i: (b * nq + i, p)),
        out_shape=jax.ShapeDtypeStruct((m, SB_INNER), BF16),
        scratch_shapes=[
            pltpu.VMEM((nh, qt, SB_GROUP * SB_HEAD_DIM), BF16),
            pltpu.VMEM((qt, width), F32),
            pltpu.VMEM((nh, qt, 1), F32),
            pltpu.SMEM((1,), jnp.int32),
        ],
        compiler_params=pltpu.CompilerParams(
            dimension_semantics=("parallel", "parallel", "arbitrary"),
            vmem_limit_bytes=VMEM_LIMIT),
        name="stick_breaking",
    )(proj, proj, proj, proj, uu)


def _merge_kernel(yn_ref, osb_ref, g0_ref, g1_ref, g2_ref, g3_ref, x_ref, bg_ref, wssd_ref,
                  wsb_ref, wout_ref, nw_ref, out_ref):
    y_ssd = jnp.dot(yn_ref[...], wssd_ref[...], preferred_element_type=F32)
    y_sb = jnp.dot(osb_ref[...], wsb_ref[...], preferred_element_type=F32)
    gates = jnp.concatenate([g0_ref[...], g1_ref[...], g2_ref[...], g3_ref[...]], axis=1)
    g = _sigmoid(gates.astype(F32) + bg_ref[...])
    merged = (g[:, :D_MODEL] * y_ssd + g[:, D_MODEL:] * y_sb).astype(BF16)
    out = jnp.dot(merged, wout_ref[...], preferred_element_type=F32)
    ms = jnp.mean(out * out, axis=-1, keepdims=True)
    out_ref[...] = x_ref[...] + out * lax.rsqrt(ms + EPS) * nw_ref[...]


def _merge(yn, osb, proj, x2, b_gate, w_ssd, w_sb, w_out, norm_post, tm):
    m = x2.shape[0]
    rows = lambda i: (i, 0)
    const = lambda i: (0, 0)
    gate = lambda k: pl.BlockSpec(
        (None, tm, PROJ_CHUNK), lambda i: (COL_GATE // PROJ_CHUNK + k, i, 0))
    return pl.pallas_call(
        _merge_kernel,
        grid=(m // tm,),
        in_specs=[
            pl.BlockSpec((tm, D_MODEL), rows),
            pl.BlockSpec((tm, D_MODEL), rows),
            gate(0), gate(1), gate(2), gate(3),
            pl.BlockSpec((tm, D_MODEL), rows),
            pl.BlockSpec((1, 2 * D_MODEL), const),
            pl.BlockSpec((D_MODEL, D_MODEL), const),
            pl.BlockSpec((D_MODEL, D_MODEL), const),
            pl.BlockSpec((D_MODEL, D_MODEL), const),
            pl.BlockSpec((1, D_MODEL), const),
        ],
        out_specs=pl.BlockSpec((tm, D_MODEL), rows),
        out_shape=jax.ShapeDtypeStruct((m, D_MODEL), F32),
        compiler_params=pltpu.CompilerParams(
            dimension_semantics=("parallel",), vmem_limit_bytes=VMEM_LIMIT),
        name="merge",
    )(yn, osb, proj, proj, proj, proj, x2, b_gate, w_ssd, w_sb, w_out, norm_post)


def _layer(x, norm_pre, w_in, b_gate, conv_w, conv_b, dt_bias, a_log, d_skip, ssd_norm,
           w_ssd_proj, w_sb_proj, w_out, norm_post):
    bsz, seq, _ = x.shape
    m = bsz * seq
    x2 = x.reshape(m, D_MODEL)

    w_r = jnp.concatenate([w_in[:, :_SRC_DT], w_in[:, _SRC_DT + SSD_HEADS:]], axis=1).astype(BF16)
    w_dt = jnp.pad(w_in[:, _SRC_DT:_SRC_DT + SSD_HEADS],
                   ((0, 0), (0, LANES - SSD_HEADS))).astype(BF16)
    pad_heads = lambda v: jnp.pad(v, (0, LANES - SSD_HEADS)).reshape(1, LANES)

    tm = min(MERGE_TM, m)
    proj, dt_raw = _inproj(x2, norm_pre.reshape(1, D_MODEL), w_r, w_dt, min(INPROJ_TM, m))

    yn = _ssd(proj, dt_raw, conv_w, conv_b.reshape(1, CONV_DIM), pad_heads(dt_bias),
              pad_heads(a_log), jnp.repeat(d_skip, SSD_HEAD_DIM).reshape(1, SSD_INNER),
              ssd_norm.reshape(1, SSD_INNER), bsz, seq, SSD_CHUNK,
              min(SSD_SUBCHUNKS, seq // SSD_CHUNK))

    osb = _stick_breaking(proj, jnp.asarray(_suffix_sum_matrix(), BF16), bsz, seq,
                          min(SB_QT, seq), SB_NG)

    out = _merge(yn, osb, proj, x2, b_gate.reshape(1, 2 * D_MODEL), w_ssd_proj.astype(BF16),
                 w_sb_proj.astype(BF16), w_out.astype(BF16), norm_post.reshape(1, D_MODEL), tm)
    return out.reshape(bsz, seq, D_MODEL)


def kernel(x, norm_pre, w_in, b_gate, conv_w, conv_b, dt_bias, a_log, d_skip, ssd_norm,
           w_ssd_proj, w_sb_proj, w_out, norm_post):
    for layer in range(norm_pre.shape[0]):
        x = _layer(x, norm_pre[layer], w_in[layer], b_gate[layer], conv_w[layer],
                   conv_b[layer], dt_bias[layer], a_log[layer], d_skip[layer],
                   ssd_norm[layer], w_ssd_proj[layer], w_sb_proj[layer], w_out[layer],
                   norm_post[layer])
    return x
```

```python
import functools

import numpy as np
import jax
import jax.numpy as jnp
from jax import lax
from jax.experimental import pallas as pl
from jax.experimental.pallas import tpu as pltpu

F32 = jnp.float32
BF16 = jnp.bfloat16
HIGHEST = lax.Precision.HIGHEST

D_MODEL = 1024
EPS = 1e-6
LANES = 128

SSD_HEADS = 16
SSD_HEAD_DIM = 64
SSD_INNER = SSD_HEADS * SSD_HEAD_DIM
SSD_GROUPS = 2
SSD_STATE = 128
SSD_CONV = 4
HEADS_PER_GROUP = SSD_HEADS // SSD_GROUPS
GROUP_WIDTH = SSD_INNER // SSD_GROUPS
BC_WIDTH = 2 * SSD_GROUPS * SSD_STATE
CONV_DIM = SSD_INNER + BC_WIDTH
SSD_CHUNK = 128
SSD_SUBCHUNKS = 4
CONV_HALO = 16

SB_HEADS = 16
SB_HEAD_DIM = 64
SB_INNER = SB_HEADS * SB_HEAD_DIM
SB_BLOCK = 128
SB_QT = 256
SB_GROUP = 4
SB_NG = 2
SB_LEAD = 2
SB_SATURATED = 105.0
SB_SOFTPLUS_LINEAR = 40.0

COL_ZSSD = 0
COL_XS = COL_ZSSD + SSD_INNER
COL_BC = COL_XS + SSD_INNER
COL_Q = COL_BC + BC_WIDTH
COL_K = COL_Q + SB_INNER
COL_V = COL_K + SB_INNER
COL_ZSB = COL_V + SB_INNER
COL_GATE = COL_ZSB + SB_INNER
PROJ_COLS = COL_GATE + 2 * D_MODEL
_SRC_DT = COL_Q
PROJ_CHUNK = 512

VMEM_LIMIT = 56 * 1024 * 1024
INPROJ_TM = 2048
MERGE_TM = 1024


def _softplus(x):
    return jnp.maximum(x, 0.0) + jnp.log1p(jnp.exp(-jnp.abs(x)))


def _sigmoid(x):
    return 1.0 / (1.0 + jnp.exp(-x))


def _silu(x):
    half = 0.5 * x
    return half + half * jnp.tanh(half)


def _inproj_kernel(x_ref, nw_ref, w_ref, wdt_ref, proj_ref, dt_ref, h_ref):
    @pl.when(pl.program_id(1) == 0)
    def _():
        x = x_ref[...]
        ms = jnp.mean(x * x, axis=-1, keepdims=True)
        h = (x * lax.rsqrt(ms + EPS) * nw_ref[...]).astype(BF16)
        h_ref[...] = h
        dt_ref[...] = jnp.dot(h, wdt_ref[...], preferred_element_type=F32)

    tn = PROJ_CHUNK
    w = w_ref[:, pl.ds(pl.multiple_of(pl.program_id(1) * tn, tn), tn)]
    proj_ref[...] = jnp.dot(h_ref[...], w, preferred_element_type=F32).astype(BF16)


def _inproj(x2, norm_pre, w_r, w_dt, tm):
    m = x2.shape[0]
    tn = PROJ_CHUNK
    return pl.pallas_call(
        _inproj_kernel,
        grid=(m // tm, PROJ_COLS // tn),
        in_specs=[
            pl.BlockSpec((tm, D_MODEL), lambda i, j: (i, 0)),
            pl.BlockSpec((1, D_MODEL), lambda i, j: (0, 0)),
            pl.BlockSpec((D_MODEL, PROJ_COLS), lambda i, j: (0, 0), pipeline_mode=pl.Buffered(1)),
            pl.BlockSpec((D_MODEL, LANES), lambda i, j: (0, 0), pipeline_mode=pl.Buffered(1)),
        ],
        out_specs=[
            pl.BlockSpec((None, tm, tn), lambda i, j: (j, i, 0)),
            pl.BlockSpec((tm, LANES), lambda i, j: (i, 0)),
        ],
        out_shape=[
            jax.ShapeDtypeStruct((PROJ_COLS // tn, m, tn), BF16),
            jax.ShapeDtypeStruct((m, LANES), F32),
        ],
        scratch_shapes=[pltpu.VMEM((tm, D_MODEL), BF16)],
        compiler_params=pltpu.CompilerParams(
            dimension_semantics=("parallel", "arbitrary"), vmem_limit_bytes=VMEM_LIMIT),
        name="inproj",
    )(x2, norm_pre, w_r, w_dt)


def _split2(x):
    hi = x.astype(BF16)
    lo = (x - hi.astype(F32)).astype(BF16)
    return hi, lo


def _ssd_kernel(z0_ref, z1_ref, xs0_ref, xs1_ref, bc_ref, dt_ref, cw_ref, cb_ref, dtb_ref,
                alog_ref, dsk_ref, nw_ref, yn_ref, state_ref, xe_ref, y_ref, *, t, nsub):
    c = pl.program_id(1)

    @pl.when(c == 0)
    def _():
        state_ref[...] = jnp.zeros_like(state_ref)
        xe_ref[pl.ds(nsub * t, CONV_HALO), :] = jnp.zeros((CONV_HALO, CONV_DIM), BF16)

    xe_ref[pl.ds(0, CONV_HALO), :] = xe_ref[pl.ds(nsub * t, CONV_HALO), :]
    for k, ref in enumerate((xs0_ref, xs1_ref, bc_ref)):
        xe_ref[pl.ds(CONV_HALO, nsub * t), pl.ds(k * PROJ_CHUNK, PROJ_CHUNK)] = ref[...]
    for sub in range(nsub):
        _ssd_subchunk(sub, t, (z0_ref, z1_ref), dt_ref, cw_ref, cb_ref, dtb_ref, alog_ref,
                      dsk_ref, nw_ref, yn_ref, state_ref, xe_ref, y_ref)


def _ssd_subchunk(sub, t, z_refs, dt_ref, cw_ref, cb_ref, dtb_ref, alog_ref, dsk_ref, nw_ref,
                  yn_ref, state_ref, xe_ref, y_ref):
    rows = pl.ds(sub * t, t)
    y_ref = y_ref.at[sub]
    xe = xe_ref[pl.ds(sub * t, t + CONV_HALO), :]
    out_row = lax.broadcasted_iota(jnp.int32, (t, t + CONV_HALO), 0)
    in_row = lax.broadcasted_iota(jnp.int32, (t, t + CONV_HALO), 1)
    conv = cb_ref[...] + cw_ref[pl.ds(SSD_CONV - 1, 1), :] * xe[CONV_HALO:, :].astype(F32)
    for j in range(SSD_CONV - 1):
        lag = SSD_CONV - 1 - j
        shift = jnp.where(in_row == out_row + (CONV_HALO - lag), 1.0, 0.0).astype(BF16)
        conv = conv + cw_ref[pl.ds(j, 1), :] * jnp.dot(shift, xe, preferred_element_type=F32)
    xbc = _silu(conv)
    xs = xbc[:, :SSD_INNER]

    dt = _softplus(dt_ref[rows, :] + dtb_ref[...])
    a_dt = dt * (-jnp.exp(alog_ref[...]))
    ri = lax.broadcasted_iota(jnp.int32, (t, t), 0)
    ci = lax.broadcasted_iota(jnp.int32, (t, t), 1)
    causal = ci <= ri
    tri = jnp.where(causal, 1.0, 0.0).astype(BF16)
    hi, lo = _split2(a_dt)
    lo2 = (a_dt - hi.astype(F32) - lo.astype(F32)).astype(BF16)
    a_cs = jnp.dot(jnp.concatenate([tri, tri, tri], axis=1),
                   jnp.concatenate([hi, lo, lo2], axis=0), preferred_element_type=F32)
    a_cs_t = a_cs.T
    a_last = a_cs[t - 1:t, :]

    ek = lax.broadcasted_iota(jnp.int32, (2 * LANES, SSD_INNER), 0) % LANES
    ec = lax.broadcasted_iota(jnp.int32, (2 * LANES, SSD_INNER), 1)
    expand2 = jnp.where(ec // SSD_HEAD_DIM == ek, 1.0, 0.0).astype(BF16)

    def expand(v):
        return jnp.dot(jnp.concatenate(_split2(v), axis=1), expand2, preferred_element_type=F32)

    x_dt = xs * expand(dt)
    x_dt_b = x_dt.astype(BF16)
    x_ds_b = (x_dt * jnp.exp(expand(a_last - a_cs))).astype(BF16)
    ecs_e = jnp.exp(expand(a_cs))
    chunk_decay_e = expand(jnp.exp(jnp.broadcast_to(a_last, (SSD_STATE, LANES))))

    lane = lax.broadcasted_iota(jnp.int32, (t, LANES), 1)
    first_head = lane < SSD_HEAD_DIM
    for g in range(SSD_GROUPS):
        b_f = xbc[:, SSD_INNER + g * SSD_STATE:SSD_INNER + (g + 1) * SSD_STATE]
        b_g = b_f.astype(BF16)
        b_t = b_f.T.astype(BF16)
        c_g = xbc[:, SSD_INNER + (SSD_GROUPS + g) * SSD_STATE:
                  SSD_INNER + (SSD_GROUPS + g + 1) * SSD_STATE].astype(BF16)
        scores = lax.dot_general(c_g, b_g, (((1,), (1,)), ((), ())), preferred_element_type=F32)
        for pr in range(HEADS_PER_GROUP // 2):
            col0 = g * GROUP_WIDTH + pr * LANES
            x_pair = x_dt_b[:, col0:col0 + LANES]
            halves = []
            for hh in range(2):
                h = g * HEADS_PER_GROUP + pr * 2 + hh
                seg = a_cs[:, h:h + 1] - a_cs_t[h:h + 1, :]
                decay = jnp.exp(jnp.where(causal, seg, -jnp.inf))
                attn = (scores * decay).astype(BF16)
                halves.append(jnp.dot(attn, x_pair, preferred_element_type=F32))
            y_ref[:, pl.ds(col0, LANES)] = jnp.where(first_head, halves[0], halves[1])

        gs = pl.ds(g * GROUP_WIDTH, GROUP_WIDTH)
        cols = slice(g * GROUP_WIDTH, (g + 1) * GROUP_WIDTH)
        st = state_ref[g]
        y_off = jnp.dot(c_g, st.astype(BF16), preferred_element_type=F32)
        y_ref[:, gs] = y_ref[:, gs] + y_off * ecs_e[:, cols]
        upd = jnp.dot(b_t, x_ds_b[:, cols], preferred_element_type=F32)
        state_ref[g] = st * chunk_decay_e[:, cols] + upd

    y = y_ref[...] + xs * dsk_ref[...]
    z = jnp.concatenate([ref[rows, :] for ref in z_refs], axis=1).astype(F32)
    gated = y * _silu(z)
    for g in range(SSD_GROUPS):
        gg = gated[:, g * GROUP_WIDTH:(g + 1) * GROUP_WIDTH]
        ms = jnp.mean(gg * gg, axis=-1, keepdims=True)
        yn_ref[rows, pl.ds(g * GROUP_WIDTH, GROUP_WIDTH)] = (
            gg * lax.rsqrt(ms + EPS) * nw_ref[:, pl.ds(g * GROUP_WIDTH, GROUP_WIDTH)]).astype(BF16)


def _ssd(proj, dt_raw, conv_w, conv_b, dt_bias, a_log, d_skip_e, ssd_norm, bsz, seq, t, nsub):
    m = bsz * seq
    rows = t * nsub
    nc = seq // rows
    row = lambda b, c: b * nc + c
    const = lambda b, c: (0, 0)
    chunk = lambda col: pl.BlockSpec(
        (None, rows, PROJ_CHUNK), lambda b, c: (col // PROJ_CHUNK, row(b, c), 0))
    return pl.pallas_call(
        functools.partial(_ssd_kernel, t=t, nsub=nsub),
        grid=(bsz, nc),
        in_specs=[
            chunk(COL_ZSSD), chunk(COL_ZSSD + PROJ_CHUNK),
            chunk(COL_XS), chunk(COL_XS + PROJ_CHUNK), chunk(COL_BC),
            pl.BlockSpec((rows, LANES), lambda b, c: (row(b, c), 0)),
            pl.BlockSpec((SSD_CONV, CONV_DIM), const),
            pl.BlockSpec((1, CONV_DIM), const),
            pl.BlockSpec((1, LANES), const),
            pl.BlockSpec((1, LANES), const),
            pl.BlockSpec((1, SSD_INNER), const),
            pl.BlockSpec((1, SSD_INNER), const),
        ],
        out_specs=pl.BlockSpec((rows, SSD_INNER), lambda b, c: (row(b, c), 0)),
        out_shape=jax.ShapeDtypeStruct((m, SSD_INNER), BF16),
        scratch_shapes=[
            pltpu.VMEM((SSD_GROUPS, SSD_STATE, GROUP_WIDTH), F32),
            pltpu.VMEM((rows + CONV_HALO, CONV_DIM), BF16),
            pltpu.VMEM((nsub, t, SSD_INNER), F32),
        ],
        compiler_params=pltpu.CompilerParams(
            dimension_semantics=("parallel", "arbitrary"), vmem_limit_bytes=VMEM_LIMIT),
        name="ssd",
    )(proj, proj, proj, proj, proj, dt_raw, conv_w, conv_b, dt_bias, a_log, d_skip_e, ssd_norm)


def _suffix_sum_matrix():
    j = np.arange(2 * SB_BLOCK)[:, None]
    n = np.arange(2 * SB_BLOCK)[None, :]
    return (j > n).astype(np.float32)


def _sb_kernel(q_ref, k_ref, v_ref, zsb_ref, uu_ref, o_ref, qm_ref, acc_ref, car_ref, gate_ref,
               flag_ref, *, qt, ng):
    tile = pl.program_id(2)
    blk = SB_BLOCK
    gw = SB_GROUP * SB_HEAD_DIM
    flag_ref[0] = jnp.int32(0)
    uu = uu_ref[...]

    def prepare():
        q_head = lax.broadcasted_iota(jnp.int32, (qt, gw), 1) // SB_HEAD_DIM
        for g in range(ng):
            q = q_ref[:, pl.ds(g * gw, gw)] * jnp.asarray(SB_HEAD_DIM ** -0.5, BF16)
            for h in range(SB_GROUP):
                qm_ref[g * SB_GROUP + h] = jnp.where(q_head == h, q, jnp.zeros_like(q))
        acc_ref[...] = jnp.zeros_like(acc_ref)
        car_ref[...] = jnp.zeros_like(car_ref)
        gate_ref[...] = _silu(zsb_ref[...].astype(F32))

    def keep_mask(spec):
        j0, _, r0, nrows, diag_block = spec
        qi = tile * qt + r0 + lax.broadcasted_iota(jnp.int32, (nrows, blk), 0)
        ki = (j0 + diag_block) * blk + lax.broadcasted_iota(jnp.int32, (nrows, blk), 1)
        return ki < qi

    def mask_diagonal(spec, x):
        nblk, diag_block = spec[1], spec[4]
        if diag_block is None:
            return x
        parts = [x[:, c * blk:(c + 1) * blk] for c in range(nblk)]
        parts[diag_block] = jnp.where(keep_mask(spec), parts[diag_block], 0.0)
        return jnp.concatenate(parts, axis=1) if nblk > 1 else parts[0]

    def key_rows(spec):
        j0, nblk = spec[0], spec[1]
        return pl.ds(pl.multiple_of(j0 * blk, blk), nblk * blk)

    def logits_stage(spec, g, h):
        k2 = k_ref[key_rows(spec), pl.ds(g * gw, gw)]
        return lax.dot_general(
            qm_ref[g * SB_GROUP + h, pl.ds(spec[2], spec[3]), :], k2, (((1,), (1,)), ((), ())),
            preferred_element_type=F32)

    def sums_stage(spec, s):
        span = spec[1] * blk
        sp = jnp.where(s > SB_SOFTPLUS_LINEAR, s, jnp.log(1.0 + jnp.exp(s)))
        terms = mask_diagonal(spec, sp).astype(BF16)
        suffix = jnp.dot(terms, uu[:span, :span], preferred_element_type=F32)
        return s - sp, suffix, suffix[:, :1] + terms[:, :1].astype(F32)

    def weights_stage(spec, g, h, log_beta, suffix, total):
        rows = pl.ds(spec[2], spec[3])
        hh = g * SB_GROUP + h
        car = car_ref[hh, rows, :]
        car_ref[hh, rows, :] = car + total
        return mask_diagonal(spec, jnp.exp(log_beta - suffix - car)).astype(BF16)

    def output_stage(spec, g, ws):
        rows = pl.ds(spec[2], spec[3])
        v_head = lax.broadcasted_iota(jnp.int32, (spec[1] * blk, gw), 1) // SB_HEAD_DIM
        v2 = v_ref[key_rows(spec), pl.ds(g * gw, gw)]
        vm = jnp.concatenate(
            [jnp.where(v_head == h, v2, jnp.zeros_like(v2)) for h in range(SB_GROUP)], axis=0)
        pv = jnp.dot(jnp.concatenate(ws, axis=1), vm, preferred_element_type=F32)
        acc_ref[rows, pl.ds(g * gw, gw)] = acc_ref[rows, pl.ds(g * gw, gw)] + pv

    def run(specs):
        items = [(spec, g, h) for spec in specs for g in range(ng) for h in range(SB_GROUP)]
        logits, mids, ws = {}, {}, []
        for i in range(len(items) + 2 * SB_LEAD):
            if i < len(items):
                logits[i] = logits_stage(*items[i])
            j = i - SB_LEAD
            if 0 <= j < len(items):
                mids[j] = sums_stage(items[j][0], logits.pop(j))
            j = i - 2 * SB_LEAD
            if 0 <= j < len(items):
                spec, g, h = items[j]
                ws.append(weights_stage(spec, g, h, *mids.pop(j)))
                if h == SB_GROUP - 1:
                    output_stage(spec, g, ws)
                    ws = []

    nsub = qt // blk
    diagonal = []
    for c in reversed(range(0, nsub, 2)):
        diagonal.append((tile * nsub + c, 2, (c + 1) * blk, qt - (c + 1) * blk, 1))
        diagonal.append((tile * nsub + c, 1, c * blk, blk, 0))

    def left_spec(it):
        return (tile * nsub - 2 * (it + 1), 2, 0, qt, None)

    def flag_saturation():
        flag_ref[0] = (jnp.min(car_ref[...]) >= SB_SATURATED).astype(jnp.int32)

    @pl.when(tile == 0)
    def _():
        prepare()
        run(diagonal)

    assert nsub == 2
    first = tile * nsub

    @pl.when(tile > 0)
    def _():
        prepare()
        run(diagonal + [(first - 1, 1, 0, qt, None), (first - 2, 1, 0, blk, None)])
        flag_saturation()

    @pl.when(jnp.logical_and(tile > 0, flag_ref[0] == 0))
    def _():
        run([(first - 2, 1, blk, qt - blk, None)])
        flag_saturation()

    n_steps = tile * (nsub // 2)

    def more(it):
        return jnp.logical_and(it < n_steps, flag_ref[0] == 0)

    def body(it):
        run([left_spec(it)])
        flag_saturation()
        return it + 1

    lax.while_loop(more, body, jnp.int32(1))
    o_ref[...] = (acc_ref[...] * gate_ref[...]).astype(BF16)


def _stick_breaking(proj, uu, bsz, seq, qt, ng):
    m = bsz * seq
    nq = seq // qt
    width = ng * SB_GROUP * SB_HEAD_DIM
    nh = ng * SB_GROUP
    assert width == PROJ_CHUNK
    return pl.pallas_call(
        functools.partial(_sb_kernel, qt=qt, ng=ng),
        grid=(bsz, SB_HEADS // nh, nq),
        in_specs=[
            pl.BlockSpec((None, qt, width), lambda b, p, i: (COL_Q // width + p, b * nq + i, 0)),
            pl.BlockSpec((None, seq, width), lambda b, p, i: (COL_K // width + p, b, 0)),
            pl.BlockSpec((None, seq, width), lambda b, p, i: (COL_V // width + p, b, 0)),
            pl.BlockSpec((None, qt, width), lambda b, p, i: (COL_ZSB // width + p, b * nq + i, 0)),
            pl.BlockSpec((2 * SB_BLOCK, 2 * SB_BLOCK), lambda b, p, i: (0, 0)),
        ],
        out_specs=pl.BlockSpec((qt, width), lambda b, p, i: (b * nq + i, p)),
        out_shape=jax.ShapeDtypeStruct((m, SB_INNER), BF16),
        scratch_shapes=[
            pltpu.VMEM((nh, qt, SB_GROUP * SB_HEAD_DIM), BF16),
            pltpu.VMEM((qt, width), F32),
            pltpu.VMEM((nh, qt, 1), F32),
            pltpu.VMEM((qt, width), F32),
            pltpu.SMEM((1,), jnp.int32),
        ],
        compiler_params=pltpu.CompilerParams(
            dimension_semantics=("parallel", "parallel", "arbitrary"),
            vmem_limit_bytes=VMEM_LIMIT),
        name="stick_breaking",
    )(proj, proj, proj, proj, uu)


def _merge_kernel(yn_ref, osb_ref, g0_ref, g1_ref, g2_ref, g3_ref, x_ref, bg_ref, wssd_ref,
                  wsb_ref, wout_ref, nw_ref, out_ref):
    y_ssd = jnp.dot(yn_ref[...], wssd_ref[...], preferred_element_type=F32)
    y_sb = jnp.dot(osb_ref[...], wsb_ref[...], preferred_element_type=F32)
    gates = jnp.concatenate([g0_ref[...], g1_ref[...], g2_ref[...], g3_ref[...]], axis=1)
    g = _sigmoid(gates.astype(F32) + bg_ref[...])
    merged = (g[:, :D_MODEL] * y_ssd + g[:, D_MODEL:] * y_sb).astype(BF16)
    out = jnp.dot(merged, wout_ref[...], preferred_element_type=F32)
    ms = jnp.mean(out * out, axis=-1, keepdims=True)
    out_ref[...] = x_ref[...] + out * lax.rsqrt(ms + EPS) * nw_ref[...]


def _merge(yn, osb, proj, x2, b_gate, w_ssd, w_sb, w_out, norm_post, tm):
    m = x2.shape[0]
    rows = lambda i: (i, 0)
    const = lambda i: (0, 0)
    gate = lambda k: pl.BlockSpec(
        (None, tm, PROJ_CHUNK), lambda i: (COL_GATE // PROJ_CHUNK + k, i, 0))
    return pl.pallas_call(
        _merge_kernel,
        grid=(m // tm,),
        in_specs=[
            pl.BlockSpec((tm, D_MODEL), rows),
            pl.BlockSpec((tm, D_MODEL), rows),
            gate(0), gate(1), gate(2), gate(3),
            pl.BlockSpec((tm, D_MODEL), rows),
            pl.BlockSpec((1, 2 * D_MODEL), const),
            pl.BlockSpec((D_MODEL, D_MODEL), const),
            pl.BlockSpec((D_MODEL, D_MODEL), const),
            pl.BlockSpec((D_MODEL, D_MODEL), const),
            pl.BlockSpec((1, D_MODEL), const),
        ],
        out_specs=pl.BlockSpec((tm, D_MODEL), rows),
        out_shape=jax.ShapeDtypeStruct((m, D_MODEL), F32),
        compiler_params=pltpu.CompilerParams(
            dimension_semantics=("parallel",), vmem_limit_bytes=VMEM_LIMIT),
        name="merge",
    )(yn, osb, proj, proj, proj, proj, x2, b_gate, w_ssd, w_sb, w_out, norm_post)


def _layer(x, norm_pre, w_in, b_gate, conv_w, conv_b, dt_bias, a_log, d_skip, ssd_norm,
           w_ssd_proj, w_sb_proj, w_out, norm_post):
    bsz, seq, _ = x.shape
    m = bsz * seq
    x2 = x.reshape(m, D_MODEL)

    w_r = jnp.concatenate([w_in[:, :_SRC_DT], w_in[:, _SRC_DT + SSD_HEADS:]], axis=1).astype(BF16)
    w_dt = jnp.pad(w_in[:, _SRC_DT:_SRC_DT + SSD_HEADS],
                   ((0, 0), (0, LANES - SSD_HEADS))).astype(BF16)
    pad_heads = lambda v: jnp.pad(v, (0, LANES - SSD_HEADS)).reshape(1, LANES)

    tm = min(MERGE_TM, m)
    proj, dt_raw = _inproj(x2, norm_pre.reshape(1, D_MODEL), w_r, w_dt, min(INPROJ_TM, m))

    yn = _ssd(proj, dt_raw, conv_w, conv_b.reshape(1, CONV_DIM), pad_heads(dt_bias),
              pad_heads(a_log), jnp.repeat(d_skip, SSD_HEAD_DIM).reshape(1, SSD_INNER),
              ssd_norm.reshape(1, SSD_INNER), bsz, seq, SSD_CHUNK,
              min(SSD_SUBCHUNKS, seq // SSD_CHUNK))

    osb = _stick_breaking(proj, jnp.asarray(_suffix_sum_matrix(), BF16), bsz, seq,
                          min(SB_QT, seq), SB_NG)

    out = _merge(yn, osb, proj, x2, b_gate.reshape(1, 2 * D_MODEL), w_ssd_proj.astype(BF16),
                 w_sb_proj.astype(BF16), w_out.astype(BF16), norm_post.reshape(1, D_MODEL), tm)
    return out.reshape(bsz, seq, D_MODEL)


def kernel(x, norm_pre, w_in, b_gate, conv_w, conv_b, dt_bias, a_log, d_skip, ssd_norm,
           w_ssd_proj, w_sb_proj, w_out, norm_post):
    for layer in range(norm_pre.shape[0]):
        x = _layer(x, norm_pre[layer], w_in[layer], b_gate[layer], conv_w[layer],
                   conv_b[layer], dt_bias[layer], a_log[layer], d_skip[layer],
                   ssd_norm[layer], w_ssd_proj[layer], w_sb_proj[layer], w_out[layer],
                   norm_post[layer])
    return x
```

```python
import functools

import numpy as np
import jax
import jax.numpy as jnp
from jax import lax
from jax.experimental import pallas as pl
from jax.experimental.pallas import tpu as pltpu

F32 = jnp.float32
BF16 = jnp.bfloat16
HIGHEST = lax.Precision.HIGHEST

D_MODEL = 1024
EPS = 1e-6
LANES = 128

SSD_HEADS = 16
SSD_HEAD_DIM = 64
SSD_INNER = SSD_HEADS * SSD_HEAD_DIM
SSD_GROUPS = 2
SSD_STATE = 128
SSD_CONV = 4
HEADS_PER_GROUP = SSD_HEADS // SSD_GROUPS
GROUP_WIDTH = SSD_INNER // SSD_GROUPS
BC_WIDTH = 2 * SSD_GROUPS * SSD_STATE
CONV_DIM = SSD_INNER + BC_WIDTH
SSD_CHUNK = 128
SSD_SUBCHUNKS = 4
CONV_HALO = 16

SB_HEADS = 16
SB_HEAD_DIM = 64
SB_INNER = SB_HEADS * SB_HEAD_DIM
SB_BLOCK = 128
SB_QT = 256
SB_GROUP = 4
SB_NG = 2
SB_LEAD = 2
SB_SATURATED = 105.0
SB_SOFTPLUS_LINEAR = 40.0

COL_ZSSD = 0
COL_XS = COL_ZSSD + SSD_INNER
COL_BC = COL_XS + SSD_INNER
COL_Q = COL_BC + BC_WIDTH
COL_K = COL_Q + SB_INNER
COL_V = COL_K + SB_INNER
COL_ZSB = COL_V + SB_INNER
COL_GATE = COL_ZSB + SB_INNER
PROJ_COLS = COL_GATE + 2 * D_MODEL
_SRC_DT = COL_Q
PROJ_CHUNK = 512

VMEM_LIMIT = 56 * 1024 * 1024
INPROJ_TM = 2048
MERGE_TM = 512


def _softplus(x):
    return jnp.maximum(x, 0.0) + jnp.log1p(jnp.exp(-jnp.abs(x)))


def _sigmoid(x):
    return 1.0 / (1.0 + jnp.exp(-x))


def _silu(x):
    half = 0.5 * x
    return half + half * jnp.tanh(half)


def _inproj_kernel(x_ref, nw_ref, wa_ref, wb_ref, proj_ref, dt_ref, h_ref):
    j = pl.program_id(1)
    tn = PROJ_CHUNK
    n_left = _SRC_DT // tn

    @pl.when(j == 0)
    def _():
        x = x_ref[...]
        ms = jnp.mean(x * x, axis=-1, keepdims=True)
        h = (x * lax.rsqrt(ms + EPS) * nw_ref[...]).astype(BF16)
        h_ref[...] = h
        first = wb_ref[:, pl.ds(0, LANES)]
        is_dt = lax.broadcasted_iota(jnp.int32, first.shape, 1) < SSD_HEADS
        dt_ref[...] = jnp.dot(h, jnp.where(is_dt, first, jnp.zeros_like(first)),
                              preferred_element_type=F32)

    @pl.when(j < n_left)
    def _():
        w = wa_ref[:, pl.ds(pl.multiple_of(j * tn, tn), tn)]
        proj_ref[...] = jnp.dot(h_ref[...], w, preferred_element_type=F32).astype(BF16)

    @pl.when(j >= n_left)
    def _():
        window = wb_ref[:, pl.ds(pl.multiple_of((j - n_left) * tn, tn), tn + LANES)]
        w = window[:, SSD_HEADS:SSD_HEADS + tn]
        proj_ref[...] = jnp.dot(h_ref[...], w, preferred_element_type=F32).astype(BF16)


def _inproj(x2, norm_pre, w_a, w_b, tm):
    m = x2.shape[0]
    tn = PROJ_CHUNK
    return pl.pallas_call(
        _inproj_kernel,
        grid=(m // tm, PROJ_COLS // tn),
        in_specs=[
            pl.BlockSpec((tm, D_MODEL), lambda i, j: (i, 0)),
            pl.BlockSpec((1, D_MODEL), lambda i, j: (0, 0)),
            pl.BlockSpec(w_a.shape, lambda i, j: (0, 0), pipeline_mode=pl.Buffered(1)),
            pl.BlockSpec(w_b.shape, lambda i, j: (0, 0), pipeline_mode=pl.Buffered(1)),
        ],
        out_specs=[
            pl.BlockSpec((None, tm, tn), lambda i, j: (j, i, 0)),
            pl.BlockSpec((tm, LANES), lambda i, j: (i, 0)),
        ],
        out_shape=[
            jax.ShapeDtypeStruct((PROJ_COLS // tn, m, tn), BF16),
            jax.ShapeDtypeStruct((m, LANES), F32),
        ],
        scratch_shapes=[pltpu.VMEM((tm, D_MODEL), BF16)],
        compiler_params=pltpu.CompilerParams(
            dimension_semantics=("parallel", "arbitrary"), vmem_limit_bytes=VMEM_LIMIT),
        name="inproj",
    )(x2, norm_pre, w_a, w_b)


def _split2(x):
    hi = x.astype(BF16)
    lo = (x - hi.astype(F32)).astype(BF16)
    return hi, lo


def _ssd_kernel(z0_ref, z1_ref, xs0_ref, xs1_ref, bc_ref, dt_ref, cw_ref, cb_ref, dtb_ref,
                alog_ref, dsk_ref, nw_ref, yn_ref, state_ref, xe_ref, y_ref, *, t, nsub):
    c = pl.program_id(1)

    @pl.when(c == 0)
    def _():
        state_ref[...] = jnp.zeros_like(state_ref)
        xe_ref[pl.ds(nsub * t, CONV_HALO), :] = jnp.zeros((CONV_HALO, CONV_DIM), BF16)

    xe_ref[pl.ds(0, CONV_HALO), :] = xe_ref[pl.ds(nsub * t, CONV_HALO), :]
    for k, ref in enumerate((xs0_ref, xs1_ref, bc_ref)):
        xe_ref[pl.ds(CONV_HALO, nsub * t), pl.ds(k * PROJ_CHUNK, PROJ_CHUNK)] = ref[...]
    for sub in range(nsub):
        _ssd_subchunk(sub, t, (z0_ref, z1_ref), dt_ref, cw_ref, cb_ref, dtb_ref, alog_ref,
                      dsk_ref, nw_ref, yn_ref, state_ref, xe_ref, y_ref)


def _ssd_subchunk(sub, t, z_refs, dt_ref, cw_ref, cb_ref, dtb_ref, alog_ref, dsk_ref, nw_ref,
                  yn_ref, state_ref, xe_ref, y_ref):
    rows = pl.ds(sub * t, t)
    y_ref = y_ref.at[sub]
    xe = xe_ref[pl.ds(sub * t, t + CONV_HALO), :]
    out_row = lax.broadcasted_iota(jnp.int32, (t, t + CONV_HALO), 0)
    in_row = lax.broadcasted_iota(jnp.int32, (t, t + CONV_HALO), 1)
    conv = cb_ref[...] + cw_ref[pl.ds(SSD_CONV - 1, 1), :] * xe[CONV_HALO:, :].astype(F32)
    for j in range(SSD_CONV - 1):
        lag = SSD_CONV - 1 - j
        shift = jnp.where(in_row == out_row + (CONV_HALO - lag), 1.0, 0.0).astype(BF16)
        conv = conv + cw_ref[pl.ds(j, 1), :] * jnp.dot(shift, xe, preferred_element_type=F32)
    xbc = _silu(conv)
    xs = xbc[:, :SSD_INNER]

    dt = _softplus(dt_ref[rows, :] + dtb_ref[...])
    a_dt = dt * (-jnp.exp(alog_ref[...]))
    ri = lax.broadcasted_iota(jnp.int32, (t, t), 0)
    ci = lax.broadcasted_iota(jnp.int32, (t, t), 1)
    causal = ci <= ri
    tri = jnp.where(causal, 1.0, 0.0).astype(BF16)
    hi, lo = _split2(a_dt)
    lo2 = (a_dt - hi.astype(F32) - lo.astype(F32)).astype(BF16)
    a_cs = jnp.dot(jnp.concatenate([tri, tri, tri], axis=1),
                   jnp.concatenate([hi, lo, lo2], axis=0), preferred_element_type=F32)
    a_cs_t = a_cs.T
    a_last = a_cs[t - 1:t, :]

    ek = lax.broadcasted_iota(jnp.int32, (2 * LANES, SSD_INNER), 0) % LANES
    ec = lax.broadcasted_iota(jnp.int32, (2 * LANES, SSD_INNER), 1)
    expand2 = jnp.where(ec // SSD_HEAD_DIM == ek, 1.0, 0.0).astype(BF16)

    def expand(v):
        return jnp.dot(jnp.concatenate(_split2(v), axis=1), expand2, preferred_element_type=F32)

    x_dt = xs * expand(dt)
    x_dt_b = x_dt.astype(BF16)
    x_ds_b = (x_dt * jnp.exp(expand(a_last - a_cs))).astype(BF16)
    ecs_e = jnp.exp(expand(a_cs))
    chunk_decay_e = expand(jnp.exp(jnp.broadcast_to(a_last, (SSD_STATE, LANES))))

    lane = lax.broadcasted_iota(jnp.int32, (t, LANES), 1)
    first_head = lane < SSD_HEAD_DIM
    for g in range(SSD_GROUPS):
        b_f = xbc[:, SSD_INNER + g * SSD_STATE:SSD_INNER + (g + 1) * SSD_STATE]
        b_g = b_f.astype(BF16)
        b_t = b_f.T.astype(BF16)
        c_g = xbc[:, SSD_INNER + (SSD_GROUPS + g) * SSD_STATE:
                  SSD_INNER + (SSD_GROUPS + g + 1) * SSD_STATE].astype(BF16)
        scores = lax.dot_general(c_g, b_g, (((1,), (1,)), ((), ())), preferred_element_type=F32)
        for pr in range(HEADS_PER_GROUP // 2):
            col0 = g * GROUP_WIDTH + pr * LANES
            x_pair = x_dt_b[:, col0:col0 + LANES]
            halves = []
            for hh in range(2):
                h = g * HEADS_PER_GROUP + pr * 2 + hh
                seg = a_cs[:, h:h + 1] - a_cs_t[h:h + 1, :]
                decay = jnp.exp(jnp.where(causal, seg, -jnp.inf))
                attn = (scores * decay).astype(BF16)
                halves.append(jnp.dot(attn, x_pair, preferred_element_type=F32))
            y_ref[:, pl.ds(col0, LANES)] = jnp.where(first_head, halves[0], halves[1])

        gs = pl.ds(g * GROUP_WIDTH, GROUP_WIDTH)
        cols = slice(g * GROUP_WIDTH, (g + 1) * GROUP_WIDTH)
        st = state_ref[g]
        y_off = jnp.dot(c_g, st.astype(BF16), preferred_element_type=F32)
        y_ref[:, gs] = y_ref[:, gs] + y_off * ecs_e[:, cols]
        upd = jnp.dot(b_t, x_ds_b[:, cols], preferred_element_type=F32)
        state_ref[g] = st * chunk_decay_e[:, cols] + upd

    y = y_ref[...] + xs * dsk_ref[...]
    z = jnp.concatenate([ref[rows, :] for ref in z_refs], axis=1).astype(F32)
    gated = y * _silu(z)
    for g in range(SSD_GROUPS):
        gg = gated[:, g * GROUP_WIDTH:(g + 1) * GROUP_WIDTH]
        ms = jnp.mean(gg * gg, axis=-1, keepdims=True)
        yn_ref[rows, pl.ds(g * GROUP_WIDTH, GROUP_WIDTH)] = (
            gg * lax.rsqrt(ms + EPS) * nw_ref[:, pl.ds(g * GROUP_WIDTH, GROUP_WIDTH)]).astype(BF16)


def _ssd(proj, dt_raw, conv_w, conv_b, dt_bias, a_log, d_skip_e, ssd_norm, bsz, seq, t, nsub):
    m = bsz * seq
    rows = t * nsub
    nc = seq // rows
    row = lambda b, c: b * nc + c
    const = lambda b, c: (0, 0)
    chunk = lambda col: pl.BlockSpec(
        (None, rows, PROJ_CHUNK), lambda b, c: (col // PROJ_CHUNK, row(b, c), 0))
    return pl.pallas_call(
        functools.partial(_ssd_kernel, t=t, nsub=nsub),
        grid=(bsz, nc),
        in_specs=[
            chunk(COL_ZSSD), chunk(COL_ZSSD + PROJ_CHUNK),
            chunk(COL_XS), chunk(COL_XS + PROJ_CHUNK), chunk(COL_BC),
            pl.BlockSpec((rows, LANES), lambda b, c: (row(b, c), 0)),
            pl.BlockSpec((SSD_CONV, CONV_DIM), const),
            pl.BlockSpec((1, CONV_DIM), const),
            pl.BlockSpec((1, LANES), const),
            pl.BlockSpec((1, LANES), const),
            pl.BlockSpec((1, SSD_INNER), const),
            pl.BlockSpec((1, SSD_INNER), const),
        ],
        out_specs=pl.BlockSpec((rows, SSD_INNER), lambda b, c: (row(b, c), 0)),
        out_shape=jax.ShapeDtypeStruct((m, SSD_INNER), BF16),
        scratch_shapes=[
            pltpu.VMEM((SSD_GROUPS, SSD_STATE, GROUP_WIDTH), F32),
            pltpu.VMEM((rows + CONV_HALO, CONV_DIM), BF16),
            pltpu.VMEM((nsub, t, SSD_INNER), F32),
        ],
        compiler_params=pltpu.CompilerParams(
            dimension_semantics=("parallel", "arbitrary"), vmem_limit_bytes=VMEM_LIMIT),
        name="ssd",
    )(proj, proj, proj, proj, proj, dt_raw, conv_w, conv_b, dt_bias, a_log, d_skip_e, ssd_norm)


def _suffix_sum_matrix():
    j = np.arange(2 * SB_BLOCK)[:, None]
    n = np.arange(2 * SB_BLOCK)[None, :]
    return (j > n).astype(np.float32)


def _sb_kernel(q_ref, k_ref, v_ref, zsb_ref, uu_ref, o_ref, qm_ref, acc_ref, car_ref, gate_ref,
               flag_ref, *, qt, ng):
    tile = pl.program_id(2)
    blk = SB_BLOCK
    gw = SB_GROUP * SB_HEAD_DIM
    flag_ref[0] = jnp.int32(0)
    uu = uu_ref[...]

    def prepare():
        q_head = lax.broadcasted_iota(jnp.int32, (qt, gw), 1) // SB_HEAD_DIM
        for g in range(ng):
            q = q_ref[:, pl.ds(g * gw, gw)] * jnp.asarray(SB_HEAD_DIM ** -0.5, BF16)
            for h in range(SB_GROUP):
                qm_ref[g * SB_GROUP + h] = jnp.where(q_head == h, q, jnp.zeros_like(q))
        acc_ref[...] = jnp.zeros_like(acc_ref)
        car_ref[...] = jnp.zeros_like(car_ref)
        gate_ref[...] = _silu(zsb_ref[...].astype(F32))

    def keep_mask(spec):
        j0, _, r0, nrows, diag_block = spec
        qi = tile * qt + r0 + lax.broadcasted_iota(jnp.int32, (nrows, blk), 0)
        ki = (j0 + diag_block) * blk + lax.broadcasted_iota(jnp.int32, (nrows, blk), 1)
        return ki < qi

    def mask_diagonal(spec, x):
        nblk, diag_block = spec[1], spec[4]
        if diag_block is None:
            return x
        parts = [x[:, c * blk:(c + 1) * blk] for c in range(nblk)]
        parts[diag_block] = jnp.where(keep_mask(spec), parts[diag_block], 0.0)
        return jnp.concatenate(parts, axis=1) if nblk > 1 else parts[0]

    def key_rows(spec):
        j0, nblk = spec[0], spec[1]
        return pl.ds(pl.multiple_of(j0 * blk, blk), nblk * blk)

    def logits_stage(spec, g, h):
        k2 = k_ref[key_rows(spec), pl.ds(g * gw, gw)]
        return lax.dot_general(
            qm_ref[g * SB_GROUP + h, pl.ds(spec[2], spec[3]), :], k2, (((1,), (1,)), ((), ())),
            preferred_element_type=F32)

    def sums_stage(spec, s):
        span = spec[1] * blk
        sp = jnp.where(s > SB_SOFTPLUS_LINEAR, s, jnp.log(1.0 + jnp.exp(s)))
        terms = mask_diagonal(spec, sp).astype(BF16)
        suffix = jnp.dot(terms, uu[:span, :span], preferred_element_type=F32)
        return s - sp, suffix, suffix[:, :1] + terms[:, :1].astype(F32)

    def weights_stage(spec, g, h, log_beta, suffix, total):
        rows = pl.ds(spec[2], spec[3])
        hh = g * SB_GROUP + h
        car = car_ref[hh, rows, :]
        car_ref[hh, rows, :] = car + total
        return mask_diagonal(spec, jnp.exp(log_beta - suffix - car)).astype(BF16)

    def output_stage(spec, g, ws):
        rows = pl.ds(spec[2], spec[3])
        v_head = lax.broadcasted_iota(jnp.int32, (spec[1] * blk, gw), 1) // SB_HEAD_DIM
        v2 = v_ref[key_rows(spec), pl.ds(g * gw, gw)]
        vm = jnp.concatenate(
            [jnp.where(v_head == h, v2, jnp.zeros_like(v2)) for h in range(SB_GROUP)], axis=0)
        pv = jnp.dot(jnp.concatenate(ws, axis=1), vm, preferred_element_type=F32)
        acc_ref[rows, pl.ds(g * gw, gw)] = acc_ref[rows, pl.ds(g * gw, gw)] + pv

    def run(specs):
        items = [(spec, g, h) for spec in specs for g in range(ng) for h in range(SB_GROUP)]
        logits, mids, ws = {}, {}, []
        for i in range(len(items) + 2 * SB_LEAD):
            if i < len(items):
                logits[i] = logits_stage(*items[i])
            j = i - SB_LEAD
            if 0 <= j < len(items):
                mids[j] = sums_stage(items[j][0], logits.pop(j))
            j = i - 2 * SB_LEAD
            if 0 <= j < len(items):
                spec, g, h = items[j]
                ws.append(weights_stage(spec, g, h, *mids.pop(j)))
                if h == SB_GROUP - 1:
                    output_stage(spec, g, ws)
                    ws = []

    nsub = qt // blk
    diagonal = []
    for c in reversed(range(0, nsub, 2)):
        diagonal.append((tile * nsub + c, 2, (c + 1) * blk, qt - (c + 1) * blk, 1))
        diagonal.append((tile * nsub + c, 1, c * blk, blk, 0))

    def left_spec(it):
        return (tile * nsub - 2 * (it + 1), 2, 0, qt, None)

    def flag_saturation():
        flag_ref[0] = (jnp.min(car_ref[...]) >= SB_SATURATED).astype(jnp.int32)

    @pl.when(tile == 0)
    def _():
        prepare()
        run(diagonal)

    assert nsub == 2
    first = tile * nsub

    @pl.when(tile > 0)
    def _():
        prepare()
        run(diagonal + [(first - 1, 1, 0, qt, None), (first - 2, 1, 0, blk, None)])
        flag_saturation()

    @pl.when(jnp.logical_and(tile > 0, flag_ref[0] == 0))
    def _():
        run([(first - 2, 1, blk, qt - blk, None)])
        flag_saturation()

    n_steps = tile * (nsub // 2)

    def more(it):
        return jnp.logical_and(it < n_steps, flag_ref[0] == 0)

    def body(it):
        run([left_spec(it)])
        flag_saturation()
        return it + 1

    lax.while_loop(more, body, jnp.int32(1))
    o_ref[...] = (acc_ref[...] * gate_ref[...]).astype(BF16)


def _stick_breaking(proj, uu, bsz, seq, qt, ng):
    m = bsz * seq
    nq = seq // qt
    width = ng * SB_GROUP * SB_HEAD_DIM
    nh = ng * SB_GROUP
    assert width == PROJ_CHUNK
    return pl.pallas_call(
        functools.partial(_sb_kernel, qt=qt, ng=ng),
        grid=(bsz, SB_HEADS // nh, nq),
        in_specs=[
            pl.BlockSpec((None, qt, width), lambda b, p, i: (COL_Q // width + p, b * nq + i, 0)),
            pl.BlockSpec((None, seq, width), lambda b, p, i: (COL_K // width + p, b, 0)),
            pl.BlockSpec((None, seq, width), lambda b, p, i: (COL_V // width + p, b, 0)),
            pl.BlockSpec((None, qt, width), lambda b, p, i: (COL_ZSB // width + p, b * nq + i, 0)),
            pl.BlockSpec((2 * SB_BLOCK, 2 * SB_BLOCK), lambda b, p, i: (0, 0)),
        ],
        out_specs=pl.BlockSpec((qt, width), lambda b, p, i: (b * nq + i, p)),
        out_shape=jax.ShapeDtypeStruct((m, SB_INNER), BF16),
        scratch_shapes=[
            pltpu.VMEM((nh, qt, SB_GROUP * SB_HEAD_DIM), BF16),
            pltpu.VMEM((qt, width), F32),
            pltpu.VMEM((nh, qt, 1), F32),
            pltpu.VMEM((qt, width), F32),
            pltpu.SMEM((1,), jnp.int32),
        ],
        compiler_params=pltpu.CompilerParams(
            dimension_semantics=("parallel", "parallel", "arbitrary"),
            vmem_limit_bytes=VMEM_LIMIT),
        name="stick_breaking",
    )(proj, proj, proj, proj, uu)


def _merge_kernel(yn_ref, osb_ref, g0_ref, g1_ref, g2_ref, g3_ref, x_ref, bg_ref, wssd_ref,
                  wsb_ref, wout_ref, nw_ref, out_ref, w_ref):
    @pl.when(pl.program_id(0) == 0)
    def _():
        for k, ref in enumerate((wssd_ref, wsb_ref, wout_ref)):
            w_ref[k] = ref[...].astype(BF16)

    y_ssd = jnp.dot(yn_ref[...], w_ref[0], preferred_element_type=F32)
    y_sb = jnp.dot(osb_ref[...], w_ref[1], preferred_element_type=F32)
    gates = jnp.concatenate([g0_ref[...], g1_ref[...], g2_ref[...], g3_ref[...]], axis=1)
    g = _sigmoid(gates.astype(F32) + bg_ref[...])
    merged = (g[:, :D_MODEL] * y_ssd + g[:, D_MODEL:] * y_sb).astype(BF16)
    out = jnp.dot(merged, w_ref[2], preferred_element_type=F32)
    ms = jnp.mean(out * out, axis=-1, keepdims=True)
    out_ref[...] = x_ref[...] + out * lax.rsqrt(ms + EPS) * nw_ref[...]


def _merge(yn, osb, proj, x2, b_gate, w_ssd, w_sb, w_out, norm_post, tm):
    m = x2.shape[0]
    rows = lambda i: (i, 0)
    const = lambda i: (0, 0)
    gate = lambda k: pl.BlockSpec(
        (None, tm, PROJ_CHUNK), lambda i: (COL_GATE // PROJ_CHUNK + k, i, 0))
    weight = pl.BlockSpec((D_MODEL, D_MODEL), const, pipeline_mode=pl.Buffered(1))
    return pl.pallas_call(
        _merge_kernel,
        grid=(m // tm,),
        in_specs=[
            pl.BlockSpec((tm, D_MODEL), rows),
            pl.BlockSpec((tm, D_MODEL), rows),
            gate(0), gate(1), gate(2), gate(3),
            pl.BlockSpec((tm, D_MODEL), rows),
            pl.BlockSpec((1, 2 * D_MODEL), const),
            weight, weight, weight,
            pl.BlockSpec((1, D_MODEL), const),
        ],
        out_specs=pl.BlockSpec((tm, D_MODEL), rows),
        out_shape=jax.ShapeDtypeStruct((m, D_MODEL), F32),
        scratch_shapes=[pltpu.VMEM((3, D_MODEL, D_MODEL), BF16)],
        compiler_params=pltpu.CompilerParams(
            dimension_semantics=("arbitrary",), vmem_limit_bytes=VMEM_LIMIT),
        name="merge",
    )(yn, osb, proj, proj, proj, proj, x2, b_gate, w_ssd, w_sb, w_out, norm_post)


def _layer(x, norm_pre, w_in, b_gate, conv_w, conv_b, dt_bias, a_log, d_skip, ssd_norm,
           w_ssd_proj, w_sb_proj, w_out, norm_post):
    bsz, seq, _ = x.shape
    m = bsz * seq
    x2 = x.reshape(m, D_MODEL)

    w_a = w_in[:, :_SRC_DT].astype(BF16)
    w_b = jnp.pad(w_in[:, _SRC_DT:], ((0, 0), (0, LANES - SSD_HEADS))).astype(BF16)
    pad_heads = lambda v: jnp.pad(v, (0, LANES - SSD_HEADS)).reshape(1, LANES)

    tm = min(MERGE_TM, m)
    proj, dt_raw = _inproj(x2, norm_pre.reshape(1, D_MODEL), w_a, w_b, min(INPROJ_TM, m))

    yn = _ssd(proj, dt_raw, conv_w, conv_b.reshape(1, CONV_DIM), pad_heads(dt_bias),
              pad_heads(a_log), jnp.repeat(d_skip, SSD_HEAD_DIM).reshape(1, SSD_INNER),
              ssd_norm.reshape(1, SSD_INNER), bsz, seq, SSD_CHUNK,
              min(SSD_SUBCHUNKS, seq // SSD_CHUNK))

    osb = _stick_breaking(proj, jnp.asarray(_suffix_sum_matrix(), BF16), bsz, seq,
                          min(SB_QT, seq), SB_NG)

    out = _merge(yn, osb, proj, x2, b_gate.reshape(1, 2 * D_MODEL), w_ssd_proj, w_sb_proj, w_out,
                 norm_post.reshape(1, D_MODEL), tm)
    return out.reshape(bsz, seq, D_MODEL)


def kernel(x, norm_pre, w_in, b_gate, conv_w, conv_b, dt_bias, a_log, d_skip, ssd_norm,
           w_ssd_proj, w_sb_proj, w_out, norm_post):
    for layer in range(norm_pre.shape[0]):
        x = _layer(x, norm_pre[layer], w_in[layer], b_gate[layer], conv_w[layer],
                   conv_b[layer], dt_bias[layer], a_log[layer], d_skip[layer],
                   ssd_norm[layer], w_ssd_proj[layer], w_sb_proj[layer], w_out[layer],
                   norm_post[layer])
    return x
```

```python
import functools

import numpy as np
import jax
import jax.numpy as jnp
from jax import lax
from jax.experimental import pallas as pl
from jax.experimental.pallas import tpu as pltpu

F32 = jnp.float32
BF16 = jnp.bfloat16
HIGHEST = lax.Precision.HIGHEST

D_MODEL = 1024
EPS = 1e-6
LANES = 128

SSD_HEADS = 16
SSD_HEAD_DIM = 64
SSD_INNER = SSD_HEADS * SSD_HEAD_DIM
SSD_GROUPS = 2
SSD_STATE = 128
SSD_CONV = 4
HEADS_PER_GROUP = SSD_HEADS // SSD_GROUPS
GROUP_WIDTH = SSD_INNER // SSD_GROUPS
BC_WIDTH = 2 * SSD_GROUPS * SSD_STATE
CONV_DIM = SSD_INNER + BC_WIDTH
SSD_CHUNK = 128
SSD_SUBCHUNKS = 4
CONV_HALO = 16

SB_HEADS = 16
SB_HEAD_DIM = 64
SB_INNER = SB_HEADS * SB_HEAD_DIM
SB_BLOCK = 128
SB_QT = 256
SB_GROUP = 4
SB_NG = 2
SB_LEAD = 2
SB_SATURATED = 105.0
SB_SOFTPLUS_LINEAR = 40.0

COL_ZSSD = 0
COL_XS = COL_ZSSD + SSD_INNER
COL_BC = COL_XS + SSD_INNER
COL_Q = COL_BC + BC_WIDTH
COL_K = COL_Q + SB_INNER
COL_V = COL_K + SB_INNER
COL_ZSB = COL_V + SB_INNER
COL_GATE = COL_ZSB + SB_INNER
PROJ_COLS = COL_GATE + 2 * D_MODEL
_SRC_DT = COL_Q
PROJ_CHUNK = 512

VMEM_LIMIT = 56 * 1024 * 1024
INPROJ_TM = 2048
MERGE_TM = 512


def _softplus(x):
    return jnp.maximum(x, 0.0) + jnp.log1p(jnp.exp(-jnp.abs(x)))


def _sigmoid(x):
    return 1.0 / (1.0 + jnp.exp(-x))


def _silu(x):
    half = 0.5 * x
    return half + half * jnp.tanh(half)


def _inproj_kernel(x_ref, nw_ref, w_ref, wdt_ref, proj_ref, dt_ref, h_ref):
    @pl.when(pl.program_id(1) == 0)
    def _():
        x = x_ref[...]
        ms = jnp.mean(x * x, axis=-1, keepdims=True)
        h = (x * lax.rsqrt(ms + EPS) * nw_ref[...]).astype(BF16)
        h_ref[...] = h
        dt_ref[...] = jnp.dot(h, wdt_ref[...], preferred_element_type=F32)

    tn = PROJ_CHUNK
    w = w_ref[:, pl.ds(pl.multiple_of(pl.program_id(1) * tn, tn), tn)]
    proj_ref[...] = jnp.dot(h_ref[...], w, preferred_element_type=F32).astype(BF16)


def _inproj(x2, norm_pre, w_r, w_dt, tm):
    m = x2.shape[0]
    tn = PROJ_CHUNK
    return pl.pallas_call(
        _inproj_kernel,
        grid=(m // tm, PROJ_COLS // tn),
        in_specs=[
            pl.BlockSpec((tm, D_MODEL), lambda i, j: (i, 0)),
            pl.BlockSpec((1, D_MODEL), lambda i, j: (0, 0)),
            pl.BlockSpec((D_MODEL, PROJ_COLS), lambda i, j: (0, 0), pipeline_mode=pl.Buffered(1)),
            pl.BlockSpec((D_MODEL, LANES), lambda i, j: (0, 0), pipeline_mode=pl.Buffered(1)),
        ],
        out_specs=[
            pl.BlockSpec((None, tm, tn), lambda i, j: (j, i, 0)),
            pl.BlockSpec((tm, LANES), lambda i, j: (i, 0)),
        ],
        out_shape=[
            jax.ShapeDtypeStruct((PROJ_COLS // tn, m, tn), BF16),
            jax.ShapeDtypeStruct((m, LANES), F32),
        ],
        scratch_shapes=[pltpu.VMEM((tm, D_MODEL), BF16)],
        compiler_params=pltpu.CompilerParams(
            dimension_semantics=("parallel", "arbitrary"), vmem_limit_bytes=VMEM_LIMIT),
        name="inproj",
    )(x2, norm_pre, w_r, w_dt)


def _split2(x):
    hi = x.astype(BF16)
    lo = (x - hi.astype(F32)).astype(BF16)
    return hi, lo


def _ssd_kernel(z0_ref, z1_ref, xs0_ref, xs1_ref, bc_ref, dt_ref, cw_ref, cb_ref, dtb_ref,
                alog_ref, dsk_ref, nw_ref, yn_ref, state_ref, xe_ref, y_ref, *, t, nsub):
    c = pl.program_id(1)

    @pl.when(c == 0)
    def _():
        state_ref[...] = jnp.zeros_like(state_ref)
        xe_ref[pl.ds(nsub * t, CONV_HALO), :] = jnp.zeros((CONV_HALO, CONV_DIM), BF16)

    xe_ref[pl.ds(0, CONV_HALO), :] = xe_ref[pl.ds(nsub * t, CONV_HALO), :]
    for k, ref in enumerate((xs0_ref, xs1_ref, bc_ref)):
        xe_ref[pl.ds(CONV_HALO, nsub * t), pl.ds(k * PROJ_CHUNK, PROJ_CHUNK)] = ref[...]
    for sub in range(nsub):
        _ssd_subchunk(sub, t, (z0_ref, z1_ref), dt_ref, cw_ref, cb_ref, dtb_ref, alog_ref,
                      dsk_ref, nw_ref, yn_ref, state_ref, xe_ref, y_ref)


def _ssd_subchunk(sub, t, z_refs, dt_ref, cw_ref, cb_ref, dtb_ref, alog_ref, dsk_ref, nw_ref,
                  yn_ref, state_ref, xe_ref, y_ref):
    rows = pl.ds(sub * t, t)
    y_ref = y_ref.at[sub]
    out_row = lax.broadcasted_iota(jnp.int32, (t, t + CONV_HALO), 0)
    in_row = lax.broadcasted_iota(jnp.int32, (t, t + CONV_HALO), 1)
    shifts = [jnp.where(in_row == out_row + (CONV_HALO - (SSD_CONV - 1 - j)), 1.0, 0.0).astype(BF16)
              for j in range(SSD_CONV - 1)]

    def conv_silu(c0, width):
        cols = pl.ds(c0, width)
        xe = xe_ref[pl.ds(sub * t, t + CONV_HALO), cols]
        conv = cb_ref[:, cols] + cw_ref[pl.ds(SSD_CONV - 1, 1), cols] * xe[CONV_HALO:, :].astype(F32)
        for j in range(SSD_CONV - 1):
            conv = conv + cw_ref[pl.ds(j, 1), cols] * jnp.dot(shifts[j], xe,
                                                              preferred_element_type=F32)
        return _silu(conv)

    dt = _softplus(dt_ref[rows, :] + dtb_ref[...])
    a_dt = dt * (-jnp.exp(alog_ref[...]))
    ri = lax.broadcasted_iota(jnp.int32, (t, t), 0)
    ci = lax.broadcasted_iota(jnp.int32, (t, t), 1)
    causal = ci <= ri
    tri = jnp.where(causal, 1.0, 0.0).astype(BF16)
    hi, lo = _split2(a_dt)
    lo2 = (a_dt - hi.astype(F32) - lo.astype(F32)).astype(BF16)
    a_cs = jnp.dot(jnp.concatenate([tri, tri, tri], axis=1),
                   jnp.concatenate([hi, lo, lo2], axis=0), preferred_element_type=F32)
    a_cs_t = a_cs.T
    a_last = a_cs[t - 1:t, :]
    dt_parts = jnp.concatenate(_split2(dt), axis=1)
    acs_parts = jnp.concatenate(_split2(a_cs), axis=1)
    tail_parts = jnp.concatenate(_split2(a_last - a_cs), axis=1)
    decay_parts = jnp.concatenate(
        _split2(jnp.exp(jnp.broadcast_to(a_last, (SSD_STATE, LANES)))), axis=1)

    bc = conv_silu(SSD_INNER, BC_WIDTH)
    lane = lax.broadcasted_iota(jnp.int32, (t, LANES), 1)
    first_head = lane < SSD_HEAD_DIM
    ek = lax.broadcasted_iota(jnp.int32, (2 * LANES, GROUP_WIDTH), 0) % LANES
    ec = lax.broadcasted_iota(jnp.int32, (2 * LANES, GROUP_WIDTH), 1)
    for g in range(SSD_GROUPS):
        gs = pl.ds(g * GROUP_WIDTH, GROUP_WIDTH)
        expand2 = jnp.where(ec // SSD_HEAD_DIM + g * HEADS_PER_GROUP == ek, 1.0, 0.0).astype(BF16)
        expand = lambda parts: jnp.dot(parts, expand2, preferred_element_type=F32)

        xs = conv_silu(g * GROUP_WIDTH, GROUP_WIDTH)
        x_dt = xs * expand(dt_parts)
        x_dt_b = x_dt.astype(BF16)
        x_ds_b = (x_dt * jnp.exp(expand(tail_parts))).astype(BF16)
        y_ref[:, gs] = xs * dsk_ref[:, gs]

        b_f = bc[:, g * SSD_STATE:(g + 1) * SSD_STATE]
        b_g = b_f.astype(BF16)
        b_t = b_f.T.astype(BF16)
        c_g = bc[:, (SSD_GROUPS + g) * SSD_STATE:(SSD_GROUPS + g + 1) * SSD_STATE].astype(BF16)
        scores = lax.dot_general(c_g, b_g, (((1,), (1,)), ((), ())), preferred_element_type=F32)
        for pr in range(HEADS_PER_GROUP // 2):
            x_pair = x_dt_b[:, pr * LANES:(pr + 1) * LANES]
            halves = []
            for hh in range(2):
                h = g * HEADS_PER_GROUP + pr * 2 + hh
                seg = a_cs[:, h:h + 1] - a_cs_t[h:h + 1, :]
                decay = jnp.exp(jnp.where(causal, seg, -jnp.inf))
                attn = (scores * decay).astype(BF16)
                halves.append(jnp.dot(attn, x_pair, preferred_element_type=F32))
            cols = pl.ds(g * GROUP_WIDTH + pr * LANES, LANES)
            y_ref[:, cols] = y_ref[:, cols] + jnp.where(first_head, halves[0], halves[1])

        st = state_ref[g]
        y_off = jnp.dot(c_g, st.astype(BF16), preferred_element_type=F32)
        y_ref[:, gs] = y_ref[:, gs] + y_off * jnp.exp(expand(acs_parts))
        upd = jnp.dot(b_t, x_ds_b, preferred_element_type=F32)
        state_ref[g] = st * expand(decay_parts) + upd

    z = jnp.concatenate([ref[rows, :] for ref in z_refs], axis=1).astype(F32)
    gated = y_ref[...] * _silu(z)
    for g in range(SSD_GROUPS):
        gg = gated[:, g * GROUP_WIDTH:(g + 1) * GROUP_WIDTH]
        ms = jnp.mean(gg * gg, axis=-1, keepdims=True)
        yn_ref[rows, pl.ds(g * GROUP_WIDTH, GROUP_WIDTH)] = (
            gg * lax.rsqrt(ms + EPS) * nw_ref[:, pl.ds(g * GROUP_WIDTH, GROUP_WIDTH)]).astype(BF16)


def _ssd(proj, dt_raw, conv_w, conv_b, dt_bias, a_log, d_skip_e, ssd_norm, bsz, seq, t, nsub):
    m = bsz * seq
    rows = t * nsub
    nc = seq // rows
    row = lambda b, c: b * nc + c
    const = lambda b, c: (0, 0)
    chunk = lambda col: pl.BlockSpec(
        (None, rows, PROJ_CHUNK), lambda b, c: (col // PROJ_CHUNK, row(b, c), 0))
    return pl.pallas_call(
        functools.partial(_ssd_kernel, t=t, nsub=nsub),
        grid=(bsz, nc),
        in_specs=[
            chunk(COL_ZSSD), chunk(COL_ZSSD + PROJ_CHUNK),
            chunk(COL_XS), chunk(COL_XS + PROJ_CHUNK), chunk(COL_BC),
            pl.BlockSpec((rows, LANES), lambda b, c: (row(b, c), 0)),
            pl.BlockSpec((SSD_CONV, CONV_DIM), const),
            pl.BlockSpec((1, CONV_DIM), const),
            pl.BlockSpec((1, LANES), const),
            pl.BlockSpec((1, LANES), const),
            pl.BlockSpec((1, SSD_INNER), const),
            pl.BlockSpec((1, SSD_INNER), const),
        ],
        out_specs=pl.BlockSpec((rows, SSD_INNER), lambda b, c: (row(b, c), 0)),
        out_shape=jax.ShapeDtypeStruct((m, SSD_INNER), BF16),
        scratch_shapes=[
            pltpu.VMEM((SSD_GROUPS, SSD_STATE, GROUP_WIDTH), F32),
            pltpu.VMEM((rows + CONV_HALO, CONV_DIM), BF16),
            pltpu.VMEM((nsub, t, SSD_INNER), F32),
        ],
        compiler_params=pltpu.CompilerParams(
            dimension_semantics=("parallel", "arbitrary"), vmem_limit_bytes=VMEM_LIMIT),
        name="ssd",
    )(proj, proj, proj, proj, proj, dt_raw, conv_w, conv_b, dt_bias, a_log, d_skip_e, ssd_norm)


def _suffix_sum_matrix():
    j = np.arange(2 * SB_BLOCK)[:, None]
    n = np.arange(2 * SB_BLOCK)[None, :]
    return (j > n).astype(np.float32)


def _sb_kernel(q_ref, k_ref, v_ref, zsb_ref, uu_ref, o_ref, qm_ref, acc_ref, car_ref, gate_ref,
               flag_ref, *, qt, ng):
    tile = pl.program_id(2)
    blk = SB_BLOCK
    gw = SB_GROUP * SB_HEAD_DIM
    flag_ref[0] = jnp.int32(0)
    uu = uu_ref[...]

    def prepare():
        q_head = lax.broadcasted_iota(jnp.int32, (qt, gw), 1) // SB_HEAD_DIM
        for g in range(ng):
            q = q_ref[:, pl.ds(g * gw, gw)] * jnp.asarray(SB_HEAD_DIM ** -0.5, BF16)
            for h in range(SB_GROUP):
                qm_ref[g * SB_GROUP + h] = jnp.where(q_head == h, q, jnp.zeros_like(q))
        acc_ref[...] = jnp.zeros_like(acc_ref)
        car_ref[...] = jnp.zeros_like(car_ref)
        gate_ref[...] = _silu(zsb_ref[...].astype(F32))

    def keep_mask(spec):
        j0, _, r0, nrows, diag_block = spec
        qi = tile * qt + r0 + lax.broadcasted_iota(jnp.int32, (nrows, blk), 0)
        ki = (j0 + diag_block) * blk + lax.broadcasted_iota(jnp.int32, (nrows, blk), 1)
        return ki < qi

    def mask_diagonal(spec, x):
        nblk, diag_block = spec[1], spec[4]
        if diag_block is None:
            return x
        parts = [x[:, c * blk:(c + 1) * blk] for c in range(nblk)]
        parts[diag_block] = jnp.where(keep_mask(spec), parts[diag_block], 0.0)
        return jnp.concatenate(parts, axis=1) if nblk > 1 else parts[0]

    def key_rows(spec):
        j0, nblk = spec[0], spec[1]
        return pl.ds(pl.multiple_of(j0 * blk, blk), nblk * blk)

    def logits_stage(spec, g, h):
        k2 = k_ref[key_rows(spec), pl.ds(g * gw, gw)]
        return lax.dot_general(
            qm_ref[g * SB_GROUP + h, pl.ds(spec[2], spec[3]), :], k2, (((1,), (1,)), ((), ())),
            preferred_element_type=F32)

    def sums_stage(spec, s):
        span = spec[1] * blk
        sp = jnp.where(s > SB_SOFTPLUS_LINEAR, s, jnp.log(1.0 + jnp.exp(s)))
        terms = mask_diagonal(spec, sp).astype(BF16)
        suffix = jnp.dot(terms, uu[:span, :span], preferred_element_type=F32)
        return s - sp, suffix, suffix[:, :1] + terms[:, :1].astype(F32)

    def weights_stage(spec, g, h, log_beta, suffix, total):
        rows = pl.ds(spec[2], spec[3])
        hh = g * SB_GROUP + h
        car = car_ref[hh, rows, :]
        car_ref[hh, rows, :] = car + total
        return mask_diagonal(spec, jnp.exp(log_beta - suffix - car)).astype(BF16)

    def output_stage(spec, g, ws):
        rows = pl.ds(spec[2], spec[3])
        v_head = lax.broadcasted_iota(jnp.int32, (spec[1] * blk, gw), 1) // SB_HEAD_DIM
        v2 = v_ref[key_rows(spec), pl.ds(g * gw, gw)]
        vm = jnp.concatenate(
            [jnp.where(v_head == h, v2, jnp.zeros_like(v2)) for h in range(SB_GROUP)], axis=0)
        pv = jnp.dot(jnp.concatenate(ws, axis=1), vm, preferred_element_type=F32)
        acc_ref[rows, pl.ds(g * gw, gw)] = acc_ref[rows, pl.ds(g * gw, gw)] + pv

    def run(specs):
        items = [(spec, g, h) for spec in specs for g in range(ng) for h in range(SB_GROUP)]
        logits, mids, ws = {}, {}, []
        for i in range(len(items) + 2 * SB_LEAD):
            if i < len(items):
                logits[i] = logits_stage(*items[i])
            j = i - SB_LEAD
            if 0 <= j < len(items):
                mids[j] = sums_stage(items[j][0], logits.pop(j))
            j = i - 2 * SB_LEAD
            if 0 <= j < len(items):
                spec, g, h = items[j]
                ws.append(weights_stage(spec, g, h, *mids.pop(j)))
                if h == SB_GROUP - 1:
                    output_stage(spec, g, ws)
                    ws = []

    nsub = qt // blk
    diagonal = []
    for c in reversed(range(0, nsub, 2)):
        diagonal.append((tile * nsub + c, 2, (c + 1) * blk, qt - (c + 1) * blk, 1))
        diagonal.append((tile * nsub + c, 1, c * blk, blk, 0))

    def left_spec(it):
        return (tile * nsub - 2 * (it + 1), 2, 0, qt, None)

    def flag_saturation():
        flag_ref[0] = (jnp.min(car_ref[...]) >= SB_SATURATED).astype(jnp.int32)

    @pl.when(tile == 0)
    def _():
        prepare()
        run(diagonal)

    assert nsub == 2
    first = tile * nsub

    @pl.when(tile > 0)
    def _():
        prepare()
        run(diagonal + [(first - 1, 1, 0, qt, None), (first - 2, 1, 0, blk, None)])
        flag_saturation()

    @pl.when(jnp.logical_and(tile > 0, flag_ref[0] == 0))
    def _():
        run([(first - 2, 1, blk, qt - blk, None)])
        flag_saturation()

    n_steps = tile * (nsub // 2)

    def more(it):
        return jnp.logical_and(it < n_steps, flag_ref[0] == 0)

    def body(it):
        run([left_spec(it)])
        flag_saturation()
        return it + 1

    lax.while_loop(more, body, jnp.int32(1))
    o_ref[...] = (acc_ref[...] * gate_ref[...]).astype(BF16)


def _stick_breaking(proj, uu, bsz, seq, qt, ng):
    m = bsz * seq
    nq = seq // qt
    width = ng * SB_GROUP * SB_HEAD_DIM
    nh = ng * SB_GROUP
    assert width == PROJ_CHUNK
    return pl.pallas_call(
        functools.partial(_sb_kernel, qt=qt, ng=ng),
        grid=(bsz, SB_HEADS // nh, nq),
        in_specs=[
            pl.BlockSpec((None, qt, width), lambda b, p, i: (COL_Q // width + p, b * nq + i, 0)),
            pl.BlockSpec((None, seq, width), lambda b, p, i: (COL_K // width + p, b, 0)),
            pl.BlockSpec((None, seq, width), lambda b, p, i: (COL_V // width + p, b, 0)),
            pl.BlockSpec((None, qt, width), lambda b, p, i: (COL_ZSB // width + p, b * nq + i, 0)),
            pl.BlockSpec((2 * SB_BLOCK, 2 * SB_BLOCK), lambda b, p, i: (0, 0)),
        ],
        out_specs=pl.BlockSpec((qt, width), lambda b, p, i: (b * nq + i, p)),
        out_shape=jax.ShapeDtypeStruct((m, SB_INNER), BF16),
        scratch_shapes=[
            pltpu.VMEM((nh, qt, SB_GROUP * SB_HEAD_DIM), BF16),
            pltpu.VMEM((qt, width), F32),
            pltpu.VMEM((nh, qt, 1), F32),
            pltpu.VMEM((qt, width), F32),
            pltpu.SMEM((1,), jnp.int32),
        ],
        compiler_params=pltpu.CompilerParams(
            dimension_semantics=("parallel", "parallel", "arbitrary"),
            vmem_limit_bytes=VMEM_LIMIT),
        name="stick_breaking",
    )(proj, proj, proj, proj, uu)


def _merge_kernel(yn_ref, osb_ref, g0_ref, g1_ref, g2_ref, g3_ref, x_ref, bg_ref, wssd_ref,
                  wsb_ref, wout_ref, nw_ref, out_ref, w_ref):
    @pl.when(pl.program_id(0) == 0)
    def _():
        for k, ref in enumerate((wssd_ref, wsb_ref, wout_ref)):
            w_ref[k] = ref[...].astype(BF16)

    y_ssd = jnp.dot(yn_ref[...], w_ref[0], preferred_element_type=F32)
    y_sb = jnp.dot(osb_ref[...], w_ref[1], preferred_element_type=F32)
    gates = jnp.concatenate([g0_ref[...], g1_ref[...], g2_ref[...], g3_ref[...]], axis=1)
    g = _sigmoid(gates.astype(F32) + bg_ref[...])
    merged = (g[:, :D_MODEL] * y_ssd + g[:, D_MODEL:] * y_sb).astype(BF16)
    out = jnp.dot(merged, w_ref[2], preferred_element_type=F32)
    ms = jnp.mean(out * out, axis=-1, keepdims=True)
    out_ref[...] = x_ref[...] + out * lax.rsqrt(ms + EPS) * nw_ref[...]


def _merge(yn, osb, proj, x2, b_gate, w_ssd, w_sb, w_out, norm_post, tm):
    m = x2.shape[0]
    rows = lambda i: (i, 0)
    const = lambda i: (0, 0)
    gate = lambda k: pl.BlockSpec(
        (None, tm, PROJ_CHUNK), lambda i: (COL_GATE // PROJ_CHUNK + k, i, 0))
    weight = pl.BlockSpec((D_MODEL, D_MODEL), const, pipeline_mode=pl.Buffered(1))
    return pl.pallas_call(
        _merge_kernel,
        grid=(m // tm,),
        in_specs=[
            pl.BlockSpec((tm, D_MODEL), rows),
            pl.BlockSpec((tm, D_MODEL), rows),
            gate(0), gate(1), gate(2), gate(3),
            pl.BlockSpec((tm, D_MODEL), rows),
            pl.BlockSpec((1, 2 * D_MODEL), const),
            weight, weight, weight,
            pl.BlockSpec((1, D_MODEL), const),
        ],
        out_specs=pl.BlockSpec((tm, D_MODEL), rows),
        out_shape=jax.ShapeDtypeStruct((m, D_MODEL), F32),
        scratch_shapes=[pltpu.VMEM((3, D_MODEL, D_MODEL), BF16)],
        compiler_params=pltpu.CompilerParams(
            dimension_semantics=("arbitrary",), vmem_limit_bytes=VMEM_LIMIT),
        name="merge",
    )(yn, osb, proj, proj, proj, proj, x2, b_gate, w_ssd, w_sb, w_out, norm_post)


def _layer(x, norm_pre, w_in, b_gate, conv_w, conv_b, dt_bias, a_log, d_skip, ssd_norm,
           w_ssd_proj, w_sb_proj, w_out, norm_post):
    bsz, seq, _ = x.shape
    m = bsz * seq
    x2 = x.reshape(m, D_MODEL)

    w_r = jnp.concatenate([w_in[:, :_SRC_DT], w_in[:, _SRC_DT + SSD_HEADS:]], axis=1).astype(BF16)
    w_dt = jnp.pad(w_in[:, _SRC_DT:_SRC_DT + SSD_HEADS],
                   ((0, 0), (0, LANES - SSD_HEADS))).astype(BF16)
    pad_heads = lambda v: jnp.pad(v, (0, LANES - SSD_HEADS)).reshape(1, LANES)

    tm = min(MERGE_TM, m)
    proj, dt_raw = _inproj(x2, norm_pre.reshape(1, D_MODEL), w_r, w_dt, min(INPROJ_TM, m))

    yn = _ssd(proj, dt_raw, conv_w, conv_b.reshape(1, CONV_DIM), pad_heads(dt_bias),
              pad_heads(a_log), jnp.repeat(d_skip, SSD_HEAD_DIM).reshape(1, SSD_INNER),
              ssd_norm.reshape(1, SSD_INNER), bsz, seq, SSD_CHUNK,
              min(SSD_SUBCHUNKS, seq // SSD_CHUNK))

    osb = _stick_breaking(proj, jnp.asarray(_suffix_sum_matrix(), BF16), bsz, seq,
                          min(SB_QT, seq), SB_NG)

    out = _merge(yn, osb, proj, x2, b_gate.reshape(1, 2 * D_MODEL), w_ssd_proj, w_sb_proj, w_out,
                 norm_post.reshape(1, D_MODEL), tm)
    return out.reshape(bsz, seq, D_MODEL)


def kernel(x, norm_pre, w_in, b_gate, conv_w, conv_b, dt_bias, a_log, d_skip, ssd_norm,
           w_ssd_proj, w_sb_proj, w_out, norm_post):
    for layer in range(norm_pre.shape[0]):
        x = _layer(x, norm_pre[layer], w_in[layer], b_gate[layer], conv_w[layer],
                   conv_b[layer], dt_bias[layer], a_log[layer], d_skip[layer],
                   ssd_norm[layer], w_ssd_proj[layer], w_sb_proj[layer], w_out[layer],
                   norm_post[layer])
    return x
```

```python
import functools

import numpy as np
import jax
import jax.numpy as jnp
from jax import lax
from jax.experimental import pallas as pl
from jax.experimental.pallas import tpu as pltpu

F32 = jnp.float32
BF16 = jnp.bfloat16

D_MODEL = 1024
EPS = 1e-6
LANES = 128

SSD_HEADS = 16
SSD_HEAD_DIM = 64
SSD_INNER = SSD_HEADS * SSD_HEAD_DIM
SSD_GROUPS = 2
SSD_STATE = 128
SSD_CONV = 4
HEADS_PER_GROUP = SSD_HEADS // SSD_GROUPS
GROUP_WIDTH = SSD_INNER // SSD_GROUPS
BC_WIDTH = 2 * SSD_GROUPS * SSD_STATE
CONV_DIM = SSD_INNER + BC_WIDTH
SSD_CHUNK = 128
SSD_SUBCHUNKS = 4
CONV_HALO = 16

SB_HEADS = 16
SB_HEAD_DIM = 64
SB_INNER = SB_HEADS * SB_HEAD_DIM
SB_BLOCK = 128
SB_QT = 256
SB_GROUP = 4
SB_NG = 2
SB_LEAD = 2
SB_SATURATED = 105.0
SB_SOFTPLUS_LINEAR = 40.0

COL_ZSSD = 0
COL_XS = COL_ZSSD + SSD_INNER
COL_BC = COL_XS + SSD_INNER
COL_Q = COL_BC + BC_WIDTH
COL_K = COL_Q + SB_INNER
COL_V = COL_K + SB_INNER
COL_ZSB = COL_V + SB_INNER
COL_GATE = COL_ZSB + SB_INNER
PROJ_COLS = COL_GATE + 2 * D_MODEL
_SRC_DT = COL_Q
PROJ_CHUNK = 512

VMEM_LIMIT = 56 * 1024 * 1024
INPROJ_TM = 2048
MERGE_TM = 512


def _softplus(x):
    return jnp.maximum(x, 0.0) + jnp.log1p(jnp.exp(-jnp.abs(x)))


def _sigmoid(x):
    return 1.0 / (1.0 + jnp.exp(-x))


def _silu(x):
    half = 0.5 * x
    return half + half * jnp.tanh(half)


def _inproj_kernel(x_ref, nw_ref, w_ref, wdt_ref, proj_ref, dt_ref, h_ref):
    @pl.when(pl.program_id(1) == 0)
    def _():
        x = x_ref[...]
        ms = jnp.mean(x * x, axis=-1, keepdims=True)
        h = (x * lax.rsqrt(ms + EPS) * nw_ref[...]).astype(BF16)
        h_ref[...] = h
        dt_ref[...] = jnp.dot(h, wdt_ref[...], preferred_element_type=F32)

    tn = PROJ_CHUNK
    w = w_ref[:, pl.ds(pl.multiple_of(pl.program_id(1) * tn, tn), tn)]
    proj_ref[...] = jnp.dot(h_ref[...], w, preferred_element_type=F32).astype(BF16)


def _inproj(x2, norm_pre, w_r, w_dt, tm):
    m = x2.shape[0]
    tn = PROJ_CHUNK
    return pl.pallas_call(
        _inproj_kernel,
        grid=(m // tm, PROJ_COLS // tn),
        in_specs=[
            pl.BlockSpec((tm, D_MODEL), lambda i, j: (i, 0)),
            pl.BlockSpec((1, D_MODEL), lambda i, j: (0, 0)),
            pl.BlockSpec((D_MODEL, PROJ_COLS), lambda i, j: (0, 0), pipeline_mode=pl.Buffered(1)),
            pl.BlockSpec((D_MODEL, LANES), lambda i, j: (0, 0), pipeline_mode=pl.Buffered(1)),
        ],
        out_specs=[
            pl.BlockSpec((None, tm, tn), lambda i, j: (j, i, 0)),
            pl.BlockSpec((tm, LANES), lambda i, j: (i, 0)),
        ],
        out_shape=[
            jax.ShapeDtypeStruct((PROJ_COLS // tn, m, tn), BF16),
            jax.ShapeDtypeStruct((m, LANES), F32),
        ],
        scratch_shapes=[pltpu.VMEM((tm, D_MODEL), BF16)],
        compiler_params=pltpu.CompilerParams(
            dimension_semantics=("parallel", "arbitrary"), vmem_limit_bytes=VMEM_LIMIT),
        name="inproj",
    )(x2, norm_pre, w_r, w_dt)


def _split2(x):
    hi = x.astype(BF16)
    lo = (x - hi.astype(F32)).astype(BF16)
    return hi, lo


def _ssd_kernel(z0_ref, z1_ref, xs0_ref, xs1_ref, bc_ref, dt_ref, cw_ref, cb_ref, dtb_ref,
                alog_ref, dsk_ref, nw_ref, yn_ref, state_ref, xe_ref, y_ref, *, t, nsub):
    c = pl.program_id(1)

    @pl.when(c == 0)
    def _():
        state_ref[...] = jnp.zeros_like(state_ref)
        xe_ref[pl.ds(nsub * t, CONV_HALO), :] = jnp.zeros((CONV_HALO, CONV_DIM), BF16)

    xe_ref[pl.ds(0, CONV_HALO), :] = xe_ref[pl.ds(nsub * t, CONV_HALO), :]
    for k, ref in enumerate((xs0_ref, xs1_ref, bc_ref)):
        xe_ref[pl.ds(CONV_HALO, nsub * t), pl.ds(k * PROJ_CHUNK, PROJ_CHUNK)] = ref[...]
    for sub in range(nsub):
        _ssd_subchunk(sub, t, (z0_ref, z1_ref), dt_ref, cw_ref, cb_ref, dtb_ref, alog_ref,
                      dsk_ref, nw_ref, yn_ref, state_ref, xe_ref, y_ref)


def _ssd_subchunk(sub, t, z_refs, dt_ref, cw_ref, cb_ref, dtb_ref, alog_ref, dsk_ref, nw_ref,
                  yn_ref, state_ref, xe_ref, y_ref):
    rows = pl.ds(sub * t, t)
    y_ref = y_ref.at[sub]
    out_row = lax.broadcasted_iota(jnp.int32, (t, t + CONV_HALO), 0)
    in_row = lax.broadcasted_iota(jnp.int32, (t, t + CONV_HALO), 1)
    shifts = [jnp.where(in_row == out_row + (CONV_HALO - (SSD_CONV - 1 - j)), 1.0, 0.0).astype(BF16)
              for j in range(SSD_CONV - 1)]

    def conv_silu(c0, width):
        cols = pl.ds(c0, width)
        xe = xe_ref[pl.ds(sub * t, t + CONV_HALO), cols]
        conv = cb_ref[:, cols] + cw_ref[pl.ds(SSD_CONV - 1, 1), cols] * xe[CONV_HALO:, :].astype(F32)
        for j in range(SSD_CONV - 1):
            conv = conv + cw_ref[pl.ds(j, 1), cols] * jnp.dot(shifts[j], xe,
                                                              preferred_element_type=F32)
        return _silu(conv)

    dt = _softplus(dt_ref[rows, :] + dtb_ref[...])
    a_dt = dt * (-jnp.exp(alog_ref[...]))
    ri = lax.broadcasted_iota(jnp.int32, (t, t), 0)
    ci = lax.broadcasted_iota(jnp.int32, (t, t), 1)
    causal = ci <= ri
    tri = jnp.where(causal, 1.0, 0.0).astype(BF16)
    hi, lo = _split2(a_dt)
    lo2 = (a_dt - hi.astype(F32) - lo.astype(F32)).astype(BF16)
    a_cs = jnp.dot(jnp.concatenate([tri, tri, tri], axis=1),
                   jnp.concatenate([hi, lo, lo2], axis=0), preferred_element_type=F32)
    a_cs_t = a_cs.T
    a_last = a_cs[t - 1:t, :]
    dt_parts = jnp.concatenate(_split2(dt), axis=1)
    acs_parts = jnp.concatenate(_split2(a_cs), axis=1)
    tail_parts = jnp.concatenate(_split2(a_last - a_cs), axis=1)
    decay_parts = jnp.concatenate(
        _split2(jnp.exp(jnp.broadcast_to(a_last, (SSD_STATE, LANES)))), axis=1)

    bc = conv_silu(SSD_INNER, BC_WIDTH)
    lane = lax.broadcasted_iota(jnp.int32, (t, LANES), 1)
    first_head = lane < SSD_HEAD_DIM
    ek = lax.broadcasted_iota(jnp.int32, (2 * LANES, GROUP_WIDTH), 0) % LANES
    ec = lax.broadcasted_iota(jnp.int32, (2 * LANES, GROUP_WIDTH), 1)
    for g in range(SSD_GROUPS):
        gs = pl.ds(g * GROUP_WIDTH, GROUP_WIDTH)
        expand2 = jnp.where(ec // SSD_HEAD_DIM + g * HEADS_PER_GROUP == ek, 1.0, 0.0).astype(BF16)
        expand = lambda parts: jnp.dot(parts, expand2, preferred_element_type=F32)

        xs = conv_silu(g * GROUP_WIDTH, GROUP_WIDTH)
        x_dt = xs * expand(dt_parts)
        x_dt_b = x_dt.astype(BF16)
        x_ds_b = (x_dt * jnp.exp(expand(tail_parts))).astype(BF16)
        y_ref[:, gs] = xs * dsk_ref[:, gs]

        b_f = bc[:, g * SSD_STATE:(g + 1) * SSD_STATE]
        b_g = b_f.astype(BF16)
        b_t = b_f.T.astype(BF16)
        c_g = bc[:, (SSD_GROUPS + g) * SSD_STATE:(SSD_GROUPS + g + 1) * SSD_STATE].astype(BF16)
        scores = lax.dot_general(c_g, b_g, (((1,), (1,)), ((), ())), preferred_element_type=F32)
        for pr in range(HEADS_PER_GROUP // 2):
            x_pair = x_dt_b[:, pr * LANES:(pr + 1) * LANES]
            halves = []
            for hh in range(2):
                h = g * HEADS_PER_GROUP + pr * 2 + hh
                seg = a_cs[:, h:h + 1] - a_cs_t[h:h + 1, :]
                decay = jnp.exp(jnp.where(causal, seg, -jnp.inf))
                attn = (scores * decay).astype(BF16)
                halves.append(jnp.dot(attn, x_pair, preferred_element_type=F32))
            cols = pl.ds(g * GROUP_WIDTH + pr * LANES, LANES)
            y_ref[:, cols] = y_ref[:, cols] + jnp.where(first_head, halves[0], halves[1])

        st = state_ref[g]
        y_off = jnp.dot(c_g, st.astype(BF16), preferred_element_type=F32)
        y_ref[:, gs] = y_ref[:, gs] + y_off * jnp.exp(expand(acs_parts))
        upd = jnp.dot(b_t, x_ds_b, preferred_element_type=F32)
        state_ref[g] = st * expand(decay_parts) + upd

    z = jnp.concatenate([ref[rows, :] for ref in z_refs], axis=1).astype(F32)
    gated = y_ref[...] * _silu(z)
    for g in range(SSD_GROUPS):
        gg = gated[:, g * GROUP_WIDTH:(g + 1) * GROUP_WIDTH]
        ms = jnp.mean(gg * gg, axis=-1, keepdims=True)
        yn_ref[rows, pl.ds(g * GROUP_WIDTH, GROUP_WIDTH)] = (
            gg * lax.rsqrt(ms + EPS) * nw_ref[:, pl.ds(g * GROUP_WIDTH, GROUP_WIDTH)]).astype(BF16)


def _ssd(proj, dt_raw, conv_w, conv_b, dt_bias, a_log, d_skip_e, ssd_norm, bsz, seq, t, nsub):
    m = bsz * seq
    rows = t * nsub
    nc = seq // rows
    row = lambda b, c: b * nc + c
    const = lambda b, c: (0, 0)
    chunk = lambda col: pl.BlockSpec(
        (None, rows, PROJ_CHUNK), lambda b, c: (col // PROJ_CHUNK, row(b, c), 0))
    return pl.pallas_call(
        functools.partial(_ssd_kernel, t=t, nsub=nsub),
        grid=(bsz, nc),
        in_specs=[
            chunk(COL_ZSSD), chunk(COL_ZSSD + PROJ_CHUNK),
            chunk(COL_XS), chunk(COL_XS + PROJ_CHUNK), chunk(COL_BC),
            pl.BlockSpec((rows, LANES), lambda b, c: (row(b, c), 0)),
            pl.BlockSpec((SSD_CONV, CONV_DIM), const),
            pl.BlockSpec((1, CONV_DIM), const),
            pl.BlockSpec((1, LANES), const),
            pl.BlockSpec((1, LANES), const),
            pl.BlockSpec((1, SSD_INNER), const),
            pl.BlockSpec((1, SSD_INNER), const),
        ],
        out_specs=pl.BlockSpec((rows, SSD_INNER), lambda b, c: (row(b, c), 0)),
        out_shape=jax.ShapeDtypeStruct((m, SSD_INNER), BF16),
        scratch_shapes=[
            pltpu.VMEM((SSD_GROUPS, SSD_STATE, GROUP_WIDTH), F32),
            pltpu.VMEM((rows + CONV_HALO, CONV_DIM), BF16),
            pltpu.VMEM((nsub, t, SSD_INNER), F32),
        ],
        compiler_params=pltpu.CompilerParams(
            dimension_semantics=("parallel", "arbitrary"), vmem_limit_bytes=VMEM_LIMIT),
        name="ssd",
    )(proj, proj, proj, proj, proj, dt_raw, conv_w, conv_b, dt_bias, a_log, d_skip_e, ssd_norm)


def _suffix_sum_matrix():
    j = np.arange(2 * SB_BLOCK)[:, None]
    n = np.arange(2 * SB_BLOCK)[None, :]
    return (j > n).astype(np.float32)


def _sb_kernel(q_ref, k_ref, v_ref, zsb_ref, uu_ref, o_ref, qm_ref, acc_ref, car_ref, gate_ref,
               flag_ref, *, qt, ng):
    tile = pl.program_id(2)
    blk = SB_BLOCK
    gw = SB_GROUP * SB_HEAD_DIM
    flag_ref[0] = jnp.int32(0)
    uu = uu_ref[...]

    def prepare():
        q_head = lax.broadcasted_iota(jnp.int32, (qt, gw), 1) // SB_HEAD_DIM
        for g in range(ng):
            q = q_ref[:, pl.ds(g * gw, gw)] * jnp.asarray(SB_HEAD_DIM ** -0.5, BF16)
            for h in range(SB_GROUP):
                qm_ref[g * SB_GROUP + h] = jnp.where(q_head == h, q, jnp.zeros_like(q))
        acc_ref[...] = jnp.zeros_like(acc_ref)
        car_ref[...] = jnp.zeros_like(car_ref)
        gate_ref[...] = _silu(zsb_ref[...].astype(F32))

    def keep_mask(spec):
        j0, _, r0, nrows, diag_block = spec
        qi = tile * qt + r0 + lax.broadcasted_iota(jnp.int32, (nrows, blk), 0)
        ki = (j0 + diag_block) * blk + lax.broadcasted_iota(jnp.int32, (nrows, blk), 1)
        return ki < qi

    def mask_diagonal(spec, x):
        nblk, diag_block = spec[1], spec[4]
        if diag_block is None:
            return x
        parts = [x[:, c * blk:(c + 1) * blk] for c in range(nblk)]
        parts[diag_block] = jnp.where(keep_mask(spec), parts[diag_block], 0.0)
        return jnp.concatenate(parts, axis=1) if nblk > 1 else parts[0]

    def key_rows(spec):
        j0, nblk = spec[0], spec[1]
        return pl.ds(pl.multiple_of(j0 * blk, blk), nblk * blk)

    def logits_stage(spec, g, h):
        k2 = k_ref[key_rows(spec), pl.ds(g * gw, gw)]
        return lax.dot_general(
            qm_ref[g * SB_GROUP + h, pl.ds(spec[2], spec[3]), :], k2, (((1,), (1,)), ((), ())),
            preferred_element_type=F32)

    def sums_stage(spec, s):
        span = spec[1] * blk
        sp = jnp.where(s > SB_SOFTPLUS_LINEAR, s, jnp.log(1.0 + jnp.exp(s)))
        terms = mask_diagonal(spec, sp).astype(BF16)
        suffix = jnp.dot(terms, uu[:span, :span], preferred_element_type=F32)
        return s - sp, suffix, suffix[:, :1] + terms[:, :1].astype(F32)

    def weights_stage(spec, g, h, log_beta, suffix, total):
        rows = pl.ds(spec[2], spec[3])
        hh = g * SB_GROUP + h
        car = car_ref[hh, rows, :]
        car_ref[hh, rows, :] = car + total
        return mask_diagonal(spec, jnp.exp(log_beta - suffix - car)).astype(BF16)

    def output_stage(spec, g, ws):
        rows = pl.ds(spec[2], spec[3])
        v_head = lax.broadcasted_iota(jnp.int32, (spec[1] * blk, gw), 1) // SB_HEAD_DIM
        v2 = v_ref[key_rows(spec), pl.ds(g * gw, gw)]
        vm = jnp.concatenate(
            [jnp.where(v_head == h, v2, jnp.zeros_like(v2)) for h in range(SB_GROUP)], axis=0)
        pv = jnp.dot(jnp.concatenate(ws, axis=1), vm, preferred_element_type=F32)
        acc_ref[rows, pl.ds(g * gw, gw)] = acc_ref[rows, pl.ds(g * gw, gw)] + pv

    def run(specs):
        items = [(spec, g, h) for spec in specs for g in range(ng) for h in range(SB_GROUP)]
        logits, mids, ws = {}, {}, []
        for i in range(len(items) + 2 * SB_LEAD):
            if i < len(items):
                logits[i] = logits_stage(*items[i])
            j = i - SB_LEAD
            if 0 <= j < len(items):
                mids[j] = sums_stage(items[j][0], logits.pop(j))
            j = i - 2 * SB_LEAD
            if 0 <= j < len(items):
                spec, g, h = items[j]
                ws.append(weights_stage(spec, g, h, *mids.pop(j)))
                if h == SB_GROUP - 1:
                    output_stage(spec, g, ws)
                    ws = []

    nsub = qt // blk
    diagonal = []
    for c in reversed(range(0, nsub, 2)):
        diagonal.append((tile * nsub + c, 2, (c + 1) * blk, qt - (c + 1) * blk, 1))
        diagonal.append((tile * nsub + c, 1, c * blk, blk, 0))

    def left_spec(it):
        return (tile * nsub - 2 * (it + 1), 2, 0, qt, None)

    def flag_saturation():
        flag_ref[0] = (jnp.min(car_ref[...]) >= SB_SATURATED).astype(jnp.int32)

    @pl.when(tile == 0)
    def _():
        prepare()
        run(diagonal)

    assert nsub == 2
    first = tile * nsub

    @pl.when(tile > 0)
    def _():
        prepare()
        run(diagonal + [(first - 2, 2, 0, blk, None), (first - 1, 1, blk, qt - blk, None)])
        flag_saturation()

    @pl.when(jnp.logical_and(tile > 0, flag_ref[0] == 0))
    def _():
        run([(first - 2, 1, blk, qt - blk, None)])
        flag_saturation()

    n_steps = tile * (nsub // 2)

    def more(it):
        return jnp.logical_and(it < n_steps, flag_ref[0] == 0)

    def body(it):
        run([left_spec(it)])
        flag_saturation()
        return it + 1

    lax.while_loop(more, body, jnp.int32(1))
    o_ref[...] = (acc_ref[...] * gate_ref[...]).astype(BF16)


def _stick_breaking(proj, uu, bsz, seq, qt, ng):
    m = bsz * seq
    nq = seq // qt
    width = ng * SB_GROUP * SB_HEAD_DIM
    nh = ng * SB_GROUP
    assert width == PROJ_CHUNK
    return pl.pallas_call(
        functools.partial(_sb_kernel, qt=qt, ng=ng),
        grid=(bsz, SB_HEADS // nh, nq),
        in_specs=[
            pl.BlockSpec((None, qt, width), lambda b, p, i: (COL_Q // width + p, b * nq + i, 0)),
            pl.BlockSpec((None, seq, width), lambda b, p, i: (COL_K // width + p, b, 0)),
            pl.BlockSpec((None, seq, width), lambda b, p, i: (COL_V // width + p, b, 0)),
            pl.BlockSpec((None, qt, width), lambda b, p, i: (COL_ZSB // width + p, b * nq + i, 0)),
            pl.BlockSpec((2 * SB_BLOCK, 2 * SB_BLOCK), lambda b, p, i: (0, 0)),
        ],
        out_specs=pl.BlockSpec((qt, width), lambda b, p, i: (b * nq + i, p)),
        out_shape=jax.ShapeDtypeStruct((m, SB_INNER), BF16),
        scratch_shapes=[
            pltpu.VMEM((nh, qt, SB_GROUP * SB_HEAD_DIM), BF16),
            pltpu.VMEM((qt, width), F32),
            pltpu.VMEM((nh, qt, 1), F32),
            pltpu.VMEM((qt, width), F32),
            pltpu.SMEM((1,), jnp.int32),
        ],
        compiler_params=pltpu.CompilerParams(
            dimension_semantics=("parallel", "parallel", "arbitrary"),
            vmem_limit_bytes=VMEM_LIMIT),
        name="stick_breaking",
    )(proj, proj, proj, proj, uu)


def _merge_kernel(yn_ref, osb_ref, g0_ref, g1_ref, g2_ref, g3_ref, x_ref, bg_ref, wssd_ref,
                  wsb_ref, wout_ref, nw_ref, out_ref, w_ref):
    @pl.when(pl.program_id(0) == 0)
    def _():
        for k, ref in enumerate((wssd_ref, wsb_ref, wout_ref)):
            w_ref[k] = ref[...].astype(BF16)

    y_ssd = jnp.dot(yn_ref[...], w_ref[0], preferred_element_type=F32)
    y_sb = jnp.dot(osb_ref[...], w_ref[1], preferred_element_type=F32)
    gates = jnp.concatenate([g0_ref[...], g1_ref[...], g2_ref[...], g3_ref[...]], axis=1)
    g = _sigmoid(gates.astype(F32) + bg_ref[...])
    merged = (g[:, :D_MODEL] * y_ssd + g[:, D_MODEL:] * y_sb).astype(BF16)
    out = jnp.dot(merged, w_ref[2], preferred_element_type=F32)
    ms = jnp.mean(out * out, axis=-1, keepdims=True)
    out_ref[...] = x_ref[...] + out * lax.rsqrt(ms + EPS) * nw_ref[...]


def _merge(yn, osb, proj, x2, b_gate, w_ssd, w_sb, w_out, norm_post, tm):
    m = x2.shape[0]
    rows = lambda i: (i, 0)
    const = lambda i: (0, 0)
    gate = lambda k: pl.BlockSpec(
        (None, tm, PROJ_CHUNK), lambda i: (COL_GATE // PROJ_CHUNK + k, i, 0))
    weight = pl.BlockSpec((D_MODEL, D_MODEL), const, pipeline_mode=pl.Buffered(1))
    return pl.pallas_call(
        _merge_kernel,
        grid=(m // tm,),
        in_specs=[
            pl.BlockSpec((tm, D_MODEL), rows),
            pl.BlockSpec((tm, D_MODEL), rows),
            gate(0), gate(1), gate(2), gate(3),
            pl.BlockSpec((tm, D_MODEL), rows),
            pl.BlockSpec((1, 2 * D_MODEL), const),
            weight, weight, weight,
            pl.BlockSpec((1, D_MODEL), const),
        ],
        out_specs=pl.BlockSpec((tm, D_MODEL), rows),
        out_shape=jax.ShapeDtypeStruct((m, D_MODEL), F32),
        scratch_shapes=[pltpu.VMEM((3, D_MODEL, D_MODEL), BF16)],
        compiler_params=pltpu.CompilerParams(
            dimension_semantics=("arbitrary",), vmem_limit_bytes=VMEM_LIMIT),
        name="merge",
    )(yn, osb, proj, proj, proj, proj, x2, b_gate, w_ssd, w_sb, w_out, norm_post)


def _layer(x, norm_pre, w_in, b_gate, conv_w, conv_b, dt_bias, a_log, d_skip, ssd_norm,
           w_ssd_proj, w_sb_proj, w_out, norm_post):
    bsz, seq, _ = x.shape
    m = bsz * seq
    x2 = x.reshape(m, D_MODEL)

    w_r = jnp.concatenate([w_in[:, :_SRC_DT], w_in[:, _SRC_DT + SSD_HEADS:]], axis=1).astype(BF16)
    w_dt = jnp.pad(w_in[:, _SRC_DT:_SRC_DT + SSD_HEADS],
                   ((0, 0), (0, LANES - SSD_HEADS))).astype(BF16)
    pad_heads = lambda v: jnp.pad(v, (0, LANES - SSD_HEADS)).reshape(1, LANES)

    tm = min(MERGE_TM, m)
    proj, dt_raw = _inproj(x2, norm_pre.reshape(1, D_MODEL), w_r, w_dt, min(INPROJ_TM, m))

    yn = _ssd(proj, dt_raw, conv_w, conv_b.reshape(1, CONV_DIM), pad_heads(dt_bias),
              pad_heads(a_log), jnp.repeat(d_skip, SSD_HEAD_DIM).reshape(1, SSD_INNER),
              ssd_norm.reshape(1, SSD_INNER), bsz, seq, SSD_CHUNK,
              min(SSD_SUBCHUNKS, seq // SSD_CHUNK))

    osb = _stick_breaking(proj, jnp.asarray(_suffix_sum_matrix(), BF16), bsz, seq,
                          min(SB_QT, seq), SB_NG)

    out = _merge(yn, osb, proj, x2, b_gate.reshape(1, 2 * D_MODEL), w_ssd_proj, w_sb_proj, w_out,
                 norm_post.reshape(1, D_MODEL), tm)
    return out.reshape(bsz, seq, D_MODEL)


def kernel(x, norm_pre, w_in, b_gate, conv_w, conv_b, dt_bias, a_log, d_skip, ssd_norm,
           w_ssd_proj, w_sb_proj, w_out, norm_post):
    for layer in range(norm_pre.shape[0]):
        x = _layer(x, norm_pre[layer], w_in[layer], b_gate[layer], conv_w[layer],
                   conv_b[layer], dt_bias[layer], a_log[layer], d_skip[layer],
                   ssd_norm[layer], w_ssd_proj[layer], w_sb_proj[layer], w_out[layer],
                   norm_post[layer])
    return x
```

```python
import functools

import numpy as np
import jax
import jax.numpy as jnp
from jax import lax
from jax.experimental import pallas as pl
from jax.experimental.pallas import tpu as pltpu

F32 = jnp.float32
BF16 = jnp.bfloat16

D_MODEL = 1024
EPS = 1e-6
LANES = 128

SSD_HEADS = 16
SSD_HEAD_DIM = 64
SSD_INNER = SSD_HEADS * SSD_HEAD_DIM
SSD_GROUPS = 2
SSD_STATE = 128
SSD_CONV = 4
HEADS_PER_GROUP = SSD_HEADS // SSD_GROUPS
GROUP_WIDTH = SSD_INNER // SSD_GROUPS
BC_WIDTH = 2 * SSD_GROUPS * SSD_STATE
CONV_DIM = SSD_INNER + BC_WIDTH
SSD_CHUNK = 128
SSD_SUBCHUNKS = 4
CONV_HALO = 16

SB_HEADS = 16
SB_HEAD_DIM = 64
SB_INNER = SB_HEADS * SB_HEAD_DIM
SB_BLOCK = 128
SB_QT = 256
SB_GROUP = 4
SB_NG = 2
SB_LEAD = 1
SB_SATURATED = 105.0
SB_SOFTPLUS_LINEAR = 40.0

COL_ZSSD = 0
COL_XS = COL_ZSSD + SSD_INNER
COL_BC = COL_XS + SSD_INNER
COL_Q = COL_BC + BC_WIDTH
COL_K = COL_Q + SB_INNER
COL_V = COL_K + SB_INNER
COL_ZSB = COL_V + SB_INNER
COL_GATE = COL_ZSB + SB_INNER
PROJ_COLS = COL_GATE + 2 * D_MODEL
_SRC_DT = COL_Q
PROJ_CHUNK = 512

VMEM_LIMIT = 56 * 1024 * 1024
INPROJ_TM = 2048
MERGE_TM = 512


def _softplus(x):
    return jnp.maximum(x, 0.0) + jnp.log1p(jnp.exp(-jnp.abs(x)))


def _sigmoid(x):
    return 1.0 / (1.0 + jnp.exp(-x))


def _silu(x):
    half = 0.5 * x
    return half + half * jnp.tanh(half)


def _inproj_kernel(x_ref, nw_ref, w_ref, wdt_ref, proj_ref, dt_ref, h_ref):
    @pl.when(pl.program_id(1) == 0)
    def _():
        x = x_ref[...]
        ms = jnp.mean(x * x, axis=-1, keepdims=True)
        h = (x * lax.rsqrt(ms + EPS) * nw_ref[...]).astype(BF16)
        h_ref[...] = h
        dt_ref[...] = jnp.dot(h, wdt_ref[...], preferred_element_type=F32)

    tn = PROJ_CHUNK
    w = w_ref[:, pl.ds(pl.multiple_of(pl.program_id(1) * tn, tn), tn)]
    proj_ref[...] = jnp.dot(h_ref[...], w, preferred_element_type=F32).astype(BF16)


def _inproj(x2, norm_pre, w_r, w_dt, tm):
    m = x2.shape[0]
    tn = PROJ_CHUNK
    return pl.pallas_call(
        _inproj_kernel,
        grid=(m // tm, PROJ_COLS // tn),
        in_specs=[
            pl.BlockSpec((tm, D_MODEL), lambda i, j: (i, 0)),
            pl.BlockSpec((1, D_MODEL), lambda i, j: (0, 0)),
            pl.BlockSpec((D_MODEL, PROJ_COLS), lambda i, j: (0, 0), pipeline_mode=pl.Buffered(1)),
            pl.BlockSpec((D_MODEL, LANES), lambda i, j: (0, 0), pipeline_mode=pl.Buffered(1)),
        ],
        out_specs=[
            pl.BlockSpec((None, tm, tn), lambda i, j: (j, i, 0)),
            pl.BlockSpec((tm, LANES), lambda i, j: (i, 0)),
        ],
        out_shape=[
            jax.ShapeDtypeStruct((PROJ_COLS // tn, m, tn), BF16),
            jax.ShapeDtypeStruct((m, LANES), F32),
        ],
        scratch_shapes=[pltpu.VMEM((tm, D_MODEL), BF16)],
        compiler_params=pltpu.CompilerParams(
            dimension_semantics=("parallel", "arbitrary"), vmem_limit_bytes=VMEM_LIMIT),
        name="inproj",
    )(x2, norm_pre, w_r, w_dt)


def _split2(x):
    hi = x.astype(BF16)
    lo = (x - hi.astype(F32)).astype(BF16)
    return hi, lo


def _ssd_kernel(z0_ref, z1_ref, xs0_ref, xs1_ref, bc_ref, dt_ref, cw_ref, cb_ref, dtb_ref,
                alog_ref, dsk_ref, nw_ref, yn_ref, state_ref, xe_ref, y_ref, *, t, nsub):
    c = pl.program_id(1)

    @pl.when(c == 0)
    def _():
        state_ref[...] = jnp.zeros_like(state_ref)
        xe_ref[pl.ds(nsub * t, CONV_HALO), :] = jnp.zeros((CONV_HALO, CONV_DIM), BF16)

    xe_ref[pl.ds(0, CONV_HALO), :] = xe_ref[pl.ds(nsub * t, CONV_HALO), :]
    for k, ref in enumerate((xs0_ref, xs1_ref, bc_ref)):
        xe_ref[pl.ds(CONV_HALO, nsub * t), pl.ds(k * PROJ_CHUNK, PROJ_CHUNK)] = ref[...]
    for sub in range(nsub):
        _ssd_subchunk(sub, t, (z0_ref, z1_ref), dt_ref, cw_ref, cb_ref, dtb_ref, alog_ref,
                      dsk_ref, nw_ref, yn_ref, state_ref, xe_ref, y_ref)


def _ssd_subchunk(sub, t, z_refs, dt_ref, cw_ref, cb_ref, dtb_ref, alog_ref, dsk_ref, nw_ref,
                  yn_ref, state_ref, xe_ref, y_ref):
    rows = pl.ds(sub * t, t)
    y_ref = y_ref.at[sub]
    out_row = lax.broadcasted_iota(jnp.int32, (t, t + CONV_HALO), 0)
    in_row = lax.broadcasted_iota(jnp.int32, (t, t + CONV_HALO), 1)
    shifts = [jnp.where(in_row == out_row + (CONV_HALO - (SSD_CONV - 1 - j)), 1.0, 0.0).astype(BF16)
              for j in range(SSD_CONV - 1)]

    def conv_silu(c0, width):
        cols = pl.ds(c0, width)
        xe = xe_ref[pl.ds(sub * t, t + CONV_HALO), cols]
        conv = cb_ref[:, cols] + cw_ref[pl.ds(SSD_CONV - 1, 1), cols] * xe[CONV_HALO:, :].astype(F32)
        for j in range(SSD_CONV - 1):
            conv = conv + cw_ref[pl.ds(j, 1), cols] * jnp.dot(shifts[j], xe,
                                                              preferred_element_type=F32)
        return _silu(conv)

    dt = _softplus(dt_ref[rows, :] + dtb_ref[...])
    a_dt = dt * (-jnp.exp(alog_ref[...]))
    ri = lax.broadcasted_iota(jnp.int32, (t, t), 0)
    ci = lax.broadcasted_iota(jnp.int32, (t, t), 1)
    causal = ci <= ri
    tri = jnp.where(causal, 1.0, 0.0).astype(BF16)
    hi, lo = _split2(a_dt)
    lo2 = (a_dt - hi.astype(F32) - lo.astype(F32)).astype(BF16)
    a_cs = jnp.dot(jnp.concatenate([tri, tri, tri], axis=1),
                   jnp.concatenate([hi, lo, lo2], axis=0), preferred_element_type=F32)
    a_cs_t = a_cs.T
    a_last = a_cs[t - 1:t, :]
    dt_parts = jnp.concatenate(_split2(dt), axis=1)
    acs_parts = jnp.concatenate(_split2(a_cs), axis=1)
    tail_parts = jnp.concatenate(_split2(a_last - a_cs), axis=1)
    decay_parts = jnp.concatenate(
        _split2(jnp.exp(jnp.broadcast_to(a_last, (SSD_STATE, LANES)))), axis=1)

    bc = conv_silu(SSD_INNER, BC_WIDTH)
    lane = lax.broadcasted_iota(jnp.int32, (t, LANES), 1)
    first_head = lane < SSD_HEAD_DIM
    ek = lax.broadcasted_iota(jnp.int32, (2 * LANES, GROUP_WIDTH), 0) % LANES
    ec = lax.broadcasted_iota(jnp.int32, (2 * LANES, GROUP_WIDTH), 1)
    for g in range(SSD_GROUPS):
        gs = pl.ds(g * GROUP_WIDTH, GROUP_WIDTH)
        expand2 = jnp.where(ec // SSD_HEAD_DIM + g * HEADS_PER_GROUP == ek, 1.0, 0.0).astype(BF16)
        expand = lambda parts: jnp.dot(parts, expand2, preferred_element_type=F32)

        xs = conv_silu(g * GROUP_WIDTH, GROUP_WIDTH)
        x_dt = xs * expand(dt_parts)
        x_dt_b = x_dt.astype(BF16)
        x_ds_b = (x_dt * jnp.exp(expand(tail_parts))).astype(BF16)
        y_ref[:, gs] = xs * dsk_ref[:, gs]

        b_f = bc[:, g * SSD_STATE:(g + 1) * SSD_STATE]
        b_g = b_f.astype(BF16)
        b_t = b_f.T.astype(BF16)
        c_g = bc[:, (SSD_GROUPS + g) * SSD_STATE:(SSD_GROUPS + g + 1) * SSD_STATE].astype(BF16)
        scores = lax.dot_general(c_g, b_g, (((1,), (1,)), ((), ())), preferred_element_type=F32)
        for pr in range(HEADS_PER_GROUP // 2):
            x_pair = x_dt_b[:, pr * LANES:(pr + 1) * LANES]
            halves = []
            for hh in range(2):
                h = g * HEADS_PER_GROUP + pr * 2 + hh
                seg = a_cs[:, h:h + 1] - a_cs_t[h:h + 1, :]
                decay = jnp.exp(jnp.where(causal, seg, -jnp.inf))
                attn = (scores * decay).astype(BF16)
                halves.append(jnp.dot(attn, x_pair, preferred_element_type=F32))
            cols = pl.ds(g * GROUP_WIDTH + pr * LANES, LANES)
            y_ref[:, cols] = y_ref[:, cols] + jnp.where(first_head, halves[0], halves[1])

        st = state_ref[g]
        y_off = jnp.dot(c_g, st.astype(BF16), preferred_element_type=F32)
        y_ref[:, gs] = y_ref[:, gs] + y_off * jnp.exp(expand(acs_parts))
        upd = jnp.dot(b_t, x_ds_b, preferred_element_type=F32)
        state_ref[g] = st * expand(decay_parts) + upd

    z = jnp.concatenate([ref[rows, :] for ref in z_refs], axis=1).astype(F32)
    gated = y_ref[...] * _silu(z)
    for g in range(SSD_GROUPS):
        gg = gated[:, g * GROUP_WIDTH:(g + 1) * GROUP_WIDTH]
        ms = jnp.mean(gg * gg, axis=-1, keepdims=True)
        yn_ref[rows, pl.ds(g * GROUP_WIDTH, GROUP_WIDTH)] = (
            gg * lax.rsqrt(ms + EPS) * nw_ref[:, pl.ds(g * GROUP_WIDTH, GROUP_WIDTH)]).astype(BF16)


def _ssd(proj, dt_raw, conv_w, conv_b, dt_bias, a_log, d_skip_e, ssd_norm, bsz, seq, t, nsub):
    m = bsz * seq
    rows = t * nsub
    nc = seq // rows
    row = lambda b, c: b * nc + c
    const = lambda b, c: (0, 0)
    chunk = lambda col: pl.BlockSpec(
        (None, rows, PROJ_CHUNK), lambda b, c: (col // PROJ_CHUNK, row(b, c), 0))
    return pl.pallas_call(
        functools.partial(_ssd_kernel, t=t, nsub=nsub),
        grid=(bsz, nc),
        in_specs=[
            chunk(COL_ZSSD), chunk(COL_ZSSD + PROJ_CHUNK),
            chunk(COL_XS), chunk(COL_XS + PROJ_CHUNK), chunk(COL_BC),
            pl.BlockSpec((rows, LANES), lambda b, c: (row(b, c), 0)),
            pl.BlockSpec((SSD_CONV, CONV_DIM), const),
            pl.BlockSpec((1, CONV_DIM), const),
            pl.BlockSpec((1, LANES), const),
            pl.BlockSpec((1, LANES), const),
            pl.BlockSpec((1, SSD_INNER), const),
            pl.BlockSpec((1, SSD_INNER), const),
        ],
        out_specs=pl.BlockSpec((rows, SSD_INNER), lambda b, c: (row(b, c), 0)),
        out_shape=jax.ShapeDtypeStruct((m, SSD_INNER), BF16),
        scratch_shapes=[
            pltpu.VMEM((SSD_GROUPS, SSD_STATE, GROUP_WIDTH), F32),
            pltpu.VMEM((rows + CONV_HALO, CONV_DIM), BF16),
            pltpu.VMEM((nsub, t, SSD_INNER), F32),
        ],
        compiler_params=pltpu.CompilerParams(
            dimension_semantics=("parallel", "arbitrary"), vmem_limit_bytes=VMEM_LIMIT),
        name="ssd",
    )(proj, proj, proj, proj, proj, dt_raw, conv_w, conv_b, dt_bias, a_log, d_skip_e, ssd_norm)


def _suffix_sum_matrix():
    j = np.arange(2 * SB_BLOCK)[:, None]
    n = np.arange(2 * SB_BLOCK)[None, :]
    return (j > n).astype(np.float32)


def _sb_kernel(q_ref, k_ref, v_ref, zsb_ref, uu_ref, o_ref, qm_ref, acc_ref, car_ref, gate_ref,
               flag_ref, *, qt, ng):
    tile = pl.program_id(2)
    blk = SB_BLOCK
    gw = SB_GROUP * SB_HEAD_DIM
    flag_ref[0] = jnp.int32(0)
    uu = uu_ref[...]

    def prepare():
        q_head = lax.broadcasted_iota(jnp.int32, (qt, gw), 1) // SB_HEAD_DIM
        for g in range(ng):
            q = q_ref[:, pl.ds(g * gw, gw)] * jnp.asarray(SB_HEAD_DIM ** -0.5, BF16)
            for h in range(SB_GROUP):
                qm_ref[g * SB_GROUP + h] = jnp.where(q_head == h, q, jnp.zeros_like(q))
        acc_ref[...] = jnp.zeros_like(acc_ref)
        car_ref[...] = jnp.zeros_like(car_ref)
        gate_ref[...] = _silu(zsb_ref[...].astype(F32))

    def keep_mask(spec):
        j0, _, r0, nrows, diag_block = spec
        row = lax.broadcasted_iota(jnp.int32, (SB_GROUP * nrows, blk), 0) % nrows
        qi = tile * qt + r0 + row
        ki = (j0 + diag_block) * blk + lax.broadcasted_iota(jnp.int32, (SB_GROUP * nrows, blk), 1)
        return ki < qi

    def mask_diagonal(spec, x):
        nblk, diag_block = spec[1], spec[4]
        if diag_block is None:
            return x
        parts = [x[:, c * blk:(c + 1) * blk] for c in range(nblk)]
        parts[diag_block] = jnp.where(keep_mask(spec), parts[diag_block], 0.0)
        return jnp.concatenate(parts, axis=1) if nblk > 1 else parts[0]

    def key_rows(spec):
        j0, nblk = spec[0], spec[1]
        return pl.ds(pl.multiple_of(j0 * blk, blk), nblk * blk)

    def logits_stage(spec, g):
        rows = pl.ds(spec[2], spec[3])
        q4 = jnp.concatenate([qm_ref[g * SB_GROUP + h, rows, :] for h in range(SB_GROUP)], axis=0)
        k2 = k_ref[key_rows(spec), pl.ds(g * gw, gw)]
        return lax.dot_general(q4, k2, (((1,), (1,)), ((), ())),
                               preferred_element_type=F32)

    def sums_stage(spec, s):
        span = spec[1] * blk
        sp = jnp.where(s > SB_SOFTPLUS_LINEAR, s, jnp.log(1.0 + jnp.exp(s)))
        terms = mask_diagonal(spec, sp).astype(BF16)
        suffix = jnp.dot(terms, uu[:span, :span], preferred_element_type=F32)
        return s - sp, suffix, suffix[:, :1] + terms[:, :1].astype(F32)

    def output_stage(spec, g, log_beta, suffix, total):
        nrows = spec[3]
        rows = pl.ds(spec[2], nrows)
        car = jnp.concatenate([car_ref[g * SB_GROUP + h, rows, :] for h in range(SB_GROUP)], axis=0)
        new_car = car + total
        w = mask_diagonal(spec, jnp.exp(log_beta - suffix - car)).astype(BF16)
        for h in range(SB_GROUP):
            car_ref[g * SB_GROUP + h, rows, :] = new_car[h * nrows:(h + 1) * nrows]
        v_head = lax.broadcasted_iota(jnp.int32, (spec[1] * blk, gw), 1) // SB_HEAD_DIM
        v2 = v_ref[key_rows(spec), pl.ds(g * gw, gw)]
        vm = jnp.concatenate(
            [jnp.where(v_head == h, v2, jnp.zeros_like(v2)) for h in range(SB_GROUP)], axis=0)
        w_by_head = jnp.concatenate(
            [w[h * nrows:(h + 1) * nrows] for h in range(SB_GROUP)], axis=1)
        pv = jnp.dot(w_by_head, vm, preferred_element_type=F32)
        acc_ref[rows, pl.ds(g * gw, gw)] = acc_ref[rows, pl.ds(g * gw, gw)] + pv

    def run(specs):
        items = [(spec, g) for spec in specs for g in range(ng)]
        logits, mids = {}, {}
        for i in range(len(items) + 2 * SB_LEAD):
            if i < len(items):
                logits[i] = logits_stage(*items[i])
            j = i - SB_LEAD
            if 0 <= j < len(items):
                mids[j] = sums_stage(items[j][0], logits.pop(j))
            j = i - 2 * SB_LEAD
            if 0 <= j < len(items):
                output_stage(*items[j], *mids.pop(j))

    nsub = qt // blk
    diagonal = []
    for c in reversed(range(0, nsub, 2)):
        diagonal.append((tile * nsub + c, 2, (c + 1) * blk, qt - (c + 1) * blk, 1))
        diagonal.append((tile * nsub + c, 1, c * blk, blk, 0))

    def left_spec(it):
        return (tile * nsub - 2 * (it + 1), 2, 0, qt, None)

    def flag_saturation():
        flag_ref[0] = (jnp.min(car_ref[...]) >= SB_SATURATED).astype(jnp.int32)

    @pl.when(tile == 0)
    def _():
        prepare()
        run(diagonal)

    assert nsub == 2
    first = tile * nsub
    quarter = qt // 4

    @pl.when(tile > 0)
    def _():
        prepare()
        run(diagonal + [(first - 2, 2, 0, quarter, None),
                        (first - 1, 1, quarter, 2 * quarter, None)])
        flag_saturation()

    @pl.when(jnp.logical_and(tile > 0, flag_ref[0] == 0))
    def _():
        run([(first - 1, 1, 3 * quarter, quarter, None),
             (first - 2, 1, quarter, 3 * quarter, None)])
        flag_saturation()

    n_steps = tile * (nsub // 2)

    def more(it):
        return jnp.logical_and(it < n_steps, flag_ref[0] == 0)

    def body(it):
        run([left_spec(it)])
        flag_saturation()
        return it + 1

    lax.while_loop(more, body, jnp.int32(1))
    o_ref[...] = (acc_ref[...] * gate_ref[...]).astype(BF16)


def _stick_breaking(proj, uu, bsz, seq, qt, ng):
    m = bsz * seq
    nq = seq // qt
    width = ng * SB_GROUP * SB_HEAD_DIM
    nh = ng * SB_GROUP
    assert width == PROJ_CHUNK
    return pl.pallas_call(
        functools.partial(_sb_kernel, qt=qt, ng=ng),
        grid=(bsz, SB_HEADS // nh, nq),
        in_specs=[
            pl.BlockSpec((None, qt, width), lambda b, p, i: (COL_Q // width + p, b * nq + i, 0)),
            pl.BlockSpec((None, seq, width), lambda b, p, i: (COL_K // width + p, b, 0)),
            pl.BlockSpec((None, seq, width), lambda b, p, i: (COL_V // width + p, b, 0)),
            pl.BlockSpec((None, qt, width), lambda b, p, i: (COL_ZSB // width + p, b * nq + i, 0)),
            pl.BlockSpec((2 * SB_BLOCK, 2 * SB_BLOCK), lambda b, p, i: (0, 0)),
        ],
        out_specs=pl.BlockSpec((qt, width), lambda b, p, i: (b * nq + i, p)),
        out_shape=jax.ShapeDtypeStruct((m, SB_INNER), BF16),
        scratch_shapes=[
            pltpu.VMEM((nh, qt, SB_GROUP * SB_HEAD_DIM), BF16),
            pltpu.VMEM((qt, width), F32),
            pltpu.VMEM((nh, qt, 1), F32),
            pltpu.VMEM((qt, width), F32),
            pltpu.SMEM((1,), jnp.int32),
        ],
        compiler_params=pltpu.CompilerParams(
            dimension_semantics=("parallel", "parallel", "arbitrary"),
            vmem_limit_bytes=VMEM_LIMIT),
        name="stick_breaking",
    )(proj, proj, proj, proj, uu)


def _merge_kernel(yn_ref, osb_ref, g0_ref, g1_ref, g2_ref, g3_ref, x_ref, bg_ref, wssd_ref,
                  wsb_ref, wout_ref, nw_ref, out_ref, w_ref):
    @pl.when(pl.program_id(0) == 0)
    def _():
        for k, ref in enumerate((wssd_ref, wsb_ref, wout_ref)):
            w_ref[k] = ref[...].astype(BF16)

    y_ssd = jnp.dot(yn_ref[...], w_ref[0], preferred_element_type=F32)
    y_sb = jnp.dot(osb_ref[...], w_ref[1], preferred_element_type=F32)
    gates = jnp.concatenate([g0_ref[...], g1_ref[...], g2_ref[...], g3_ref[...]], axis=1)
    g = _sigmoid(gates.astype(F32) + bg_ref[...])
    merged = (g[:, :D_MODEL] * y_ssd + g[:, D_MODEL:] * y_sb).astype(BF16)
    out = jnp.dot(merged, w_ref[2], preferred_element_type=F32)
    ms = jnp.mean(out * out, axis=-1, keepdims=True)
    out_ref[...] = x_ref[...] + out * lax.rsqrt(ms + EPS) * nw_ref[...]


def _merge(yn, osb, proj, x2, b_gate, w_ssd, w_sb, w_out, norm_post, tm):
    m = x2.shape[0]
    rows = lambda i: (i, 0)
    const = lambda i: (0, 0)
    gate = lambda k: pl.BlockSpec(
        (None, tm, PROJ_CHUNK), lambda i: (COL_GATE // PROJ_CHUNK + k, i, 0))
    weight = pl.BlockSpec((D_MODEL, D_MODEL), const, pipeline_mode=pl.Buffered(1))
    return pl.pallas_call(
        _merge_kernel,
        grid=(m // tm,),
        in_specs=[
            pl.BlockSpec((tm, D_MODEL), rows),
            pl.BlockSpec((tm, D_MODEL), rows),
            gate(0), gate(1), gate(2), gate(3),
            pl.BlockSpec((tm, D_MODEL), rows),
            pl.BlockSpec((1, 2 * D_MODEL), const),
            weight, weight, weight,
            pl.BlockSpec((1, D_MODEL), const),
        ],
        out_specs=pl.BlockSpec((tm, D_MODEL), rows),
        out_shape=jax.ShapeDtypeStruct((m, D_MODEL), F32),
        scratch_shapes=[pltpu.VMEM((3, D_MODEL, D_MODEL), BF16)],
        compiler_params=pltpu.CompilerParams(
            dimension_semantics=("arbitrary",), vmem_limit_bytes=VMEM_LIMIT),
        name="merge",
    )(yn, osb, proj, proj, proj, proj, x2, b_gate, w_ssd, w_sb, w_out, norm_post)


def _layer(x, norm_pre, w_in, b_gate, conv_w, conv_b, dt_bias, a_log, d_skip, ssd_norm,
           w_ssd_proj, w_sb_proj, w_out, norm_post):
    bsz, seq, _ = x.shape
    m = bsz * seq
    x2 = x.reshape(m, D_MODEL)

    w_r = jnp.concatenate([w_in[:, :_SRC_DT], w_in[:, _SRC_DT + SSD_HEADS:]], axis=1).astype(BF16)
    w_dt = jnp.pad(w_in[:, _SRC_DT:_SRC_DT + SSD_HEADS],
                   ((0, 0), (0, LANES - SSD_HEADS))).astype(BF16)
    pad_heads = lambda v: jnp.pad(v, (0, LANES - SSD_HEADS)).reshape(1, LANES)

    tm = min(MERGE_TM, m)
    proj, dt_raw = _inproj(x2, norm_pre.reshape(1, D_MODEL), w_r, w_dt, min(INPROJ_TM, m))

    yn = _ssd(proj, dt_raw, conv_w, conv_b.reshape(1, CONV_DIM), pad_heads(dt_bias),
              pad_heads(a_log), jnp.repeat(d_skip, SSD_HEAD_DIM).reshape(1, SSD_INNER),
              ssd_norm.reshape(1, SSD_INNER), bsz, seq, SSD_CHUNK,
              min(SSD_SUBCHUNKS, seq // SSD_CHUNK))

    osb = _stick_breaking(proj, jnp.asarray(_suffix_sum_matrix(), BF16), bsz, seq,
                          min(SB_QT, seq), SB_NG)

    out = _merge(yn, osb, proj, x2, b_gate.reshape(1, 2 * D_MODEL), w_ssd_proj, w_sb_proj, w_out,
                 norm_post.reshape(1, D_MODEL), tm)
    return out.reshape(bsz, seq, D_MODEL)


def kernel(x, norm_pre, w_in, b_gate, conv_w, conv_b, dt_bias, a_log, d_skip, ssd_norm,
           w_ssd_proj, w_sb_proj, w_out, norm_post):
    for layer in range(norm_pre.shape[0]):
        x = _layer(x, norm_pre[layer], w_in[layer], b_gate[layer], conv_w[layer],
                   conv_b[layer], dt_bias[layer], a_log[layer], d_skip[layer],
                   ssd_norm[layer], w_ssd_proj[layer], w_sb_proj[layer], w_out[layer],
                   norm_post[layer])
    return x
```

```python
import functools

import numpy as np
import jax
import jax.numpy as jnp
from jax import lax
from jax.experimental import pallas as pl
from jax.experimental.pallas import tpu as pltpu

F32 = jnp.float32
BF16 = jnp.bfloat16

D_MODEL = 1024
EPS = 1e-6
LANES = 128

SSD_HEADS = 16
SSD_HEAD_DIM = 64
SSD_INNER = SSD_HEADS * SSD_HEAD_DIM
SSD_GROUPS = 2
SSD_STATE = 128
SSD_CONV = 4
HEADS_PER_GROUP = SSD_HEADS // SSD_GROUPS
GROUP_WIDTH = SSD_INNER // SSD_GROUPS
BC_WIDTH = 2 * SSD_GROUPS * SSD_STATE
CONV_DIM = SSD_INNER + BC_WIDTH
SSD_CHUNK = 128
SSD_SUBCHUNKS = 4
CONV_HALO = 16

SB_HEADS = 16
SB_HEAD_DIM = 64
SB_INNER = SB_HEADS * SB_HEAD_DIM
SB_BLOCK = 128
SB_QT = 256
SB_GROUP = 4
SB_NG = 2
SB_LEAD = 1
SB_SATURATED = 105.0
SB_SOFTPLUS_LINEAR = 40.0

COL_ZSSD = 0
COL_XS = COL_ZSSD + SSD_INNER
COL_BC = COL_XS + SSD_INNER
COL_Q = COL_BC + BC_WIDTH
COL_K = COL_Q + SB_INNER
COL_V = COL_K + SB_INNER
COL_ZSB = COL_V + SB_INNER
COL_GATE = COL_ZSB + SB_INNER
PROJ_COLS = COL_GATE + 2 * D_MODEL
_SRC_DT = COL_Q
PROJ_CHUNK = 512

VMEM_LIMIT = 56 * 1024 * 1024
INPROJ_TM = 2048
MERGE_TM = 512


def _softplus(x):
    return jnp.maximum(x, 0.0) + jnp.log1p(jnp.exp(-jnp.abs(x)))


def _sigmoid(x):
    return 1.0 / (1.0 + jnp.exp(-x))


def _silu(x):
    half = 0.5 * x
    return half + half * jnp.tanh(half)


def _inproj_kernel(x_ref, nw_ref, w_ref, wdt_ref, proj_ref, dt_ref, h_ref):
    @pl.when(pl.program_id(1) == 0)
    def _():
        x = x_ref[...]
        ms = jnp.mean(x * x, axis=-1, keepdims=True)
        h = (x * lax.rsqrt(ms + EPS) * nw_ref[...]).astype(BF16)
        h_ref[...] = h
        dt_ref[...] = jnp.dot(h, wdt_ref[...], preferred_element_type=F32)

    tn = PROJ_CHUNK
    w = w_ref[:, pl.ds(pl.multiple_of(pl.program_id(1) * tn, tn), tn)]
    proj_ref[...] = jnp.dot(h_ref[...], w, preferred_element_type=F32).astype(BF16)


def _inproj(x2, norm_pre, w_r, w_dt, tm):
    m = x2.shape[0]
    tn = PROJ_CHUNK
    return pl.pallas_call(
        _inproj_kernel,
        grid=(m // tm, PROJ_COLS // tn),
        in_specs=[
            pl.BlockSpec((tm, D_MODEL), lambda i, j: (i, 0)),
            pl.BlockSpec((1, D_MODEL), lambda i, j: (0, 0)),
            pl.BlockSpec((D_MODEL, PROJ_COLS), lambda i, j: (0, 0), pipeline_mode=pl.Buffered(1)),
            pl.BlockSpec((D_MODEL, LANES), lambda i, j: (0, 0), pipeline_mode=pl.Buffered(1)),
        ],
        out_specs=[
            pl.BlockSpec((None, tm, tn), lambda i, j: (j, i, 0)),
            pl.BlockSpec((tm, LANES), lambda i, j: (i, 0)),
        ],
        out_shape=[
            jax.ShapeDtypeStruct((PROJ_COLS // tn, m, tn), BF16),
            jax.ShapeDtypeStruct((m, LANES), F32),
        ],
        scratch_shapes=[pltpu.VMEM((tm, D_MODEL), BF16)],
        compiler_params=pltpu.CompilerParams(
            dimension_semantics=("parallel", "arbitrary"), vmem_limit_bytes=VMEM_LIMIT),
        name="inproj",
    )(x2, norm_pre, w_r, w_dt)


def _split2(x):
    hi = x.astype(BF16)
    lo = (x - hi.astype(F32)).astype(BF16)
    return hi, lo


def _ssd_kernel(z0_ref, z1_ref, xs0_ref, xs1_ref, bc_ref, dt_ref, cw_ref, cb_ref, dtb_ref,
                alog_ref, dsk_ref, nw_ref, yn_ref, state_ref, xe_ref, y_ref, *, t, nsub):
    c = pl.program_id(1)

    @pl.when(c == 0)
    def _():
        state_ref[...] = jnp.zeros_like(state_ref)
        xe_ref[pl.ds(nsub * t, CONV_HALO), :] = jnp.zeros((CONV_HALO, CONV_DIM), BF16)

    xe_ref[pl.ds(0, CONV_HALO), :] = xe_ref[pl.ds(nsub * t, CONV_HALO), :]
    for k, ref in enumerate((xs0_ref, xs1_ref, bc_ref)):
        xe_ref[pl.ds(CONV_HALO, nsub * t), pl.ds(k * PROJ_CHUNK, PROJ_CHUNK)] = ref[...]
    for sub in range(nsub):
        _ssd_subchunk(sub, t, (z0_ref, z1_ref), dt_ref, cw_ref, cb_ref, dtb_ref, alog_ref,
                      dsk_ref, nw_ref, yn_ref, state_ref, xe_ref, y_ref)


def _ssd_subchunk(sub, t, z_refs, dt_ref, cw_ref, cb_ref, dtb_ref, alog_ref, dsk_ref, nw_ref,
                  yn_ref, state_ref, xe_ref, y_ref):
    rows = pl.ds(sub * t, t)
    y_ref = y_ref.at[sub]
    out_row = lax.broadcasted_iota(jnp.int32, (t, t + CONV_HALO), 0)
    in_row = lax.broadcasted_iota(jnp.int32, (t, t + CONV_HALO), 1)
    shifts = [jnp.where(in_row == out_row + (CONV_HALO - (SSD_CONV - 1 - j)), 1.0, 0.0).astype(BF16)
              for j in range(SSD_CONV - 1)]

    def conv_silu(c0, width):
        cols = pl.ds(c0, width)
        xe = xe_ref[pl.ds(sub * t, t + CONV_HALO), cols]
        conv = cb_ref[:, cols] + cw_ref[pl.ds(SSD_CONV - 1, 1), cols] * xe[CONV_HALO:, :].astype(F32)
        for j in range(SSD_CONV - 1):
            conv = conv + cw_ref[pl.ds(j, 1), cols] * jnp.dot(shifts[j], xe,
                                                              preferred_element_type=F32)
        return _silu(conv)

    dt = _softplus(dt_ref[rows, :] + dtb_ref[...])
    a_dt = dt * (-jnp.exp(alog_ref[...]))
    ri = lax.broadcasted_iota(jnp.int32, (t, t), 0)
    ci = lax.broadcasted_iota(jnp.int32, (t, t), 1)
    causal = ci <= ri
    tri = jnp.where(causal, 1.0, 0.0).astype(BF16)
    hi, lo = _split2(a_dt)
    lo2 = (a_dt - hi.astype(F32) - lo.astype(F32)).astype(BF16)
    a_cs = jnp.dot(jnp.concatenate([tri, tri, tri], axis=1),
                   jnp.concatenate([hi, lo, lo2], axis=0), preferred_element_type=F32)
    a_cs_t = a_cs.T
    a_last = a_cs[t - 1:t, :]
    dt_parts = jnp.concatenate(_split2(dt), axis=1)
    acs_parts = jnp.concatenate(_split2(a_cs), axis=1)
    tail_parts = jnp.concatenate(_split2(a_last - a_cs), axis=1)
    decay_parts = jnp.concatenate(
        _split2(jnp.exp(jnp.broadcast_to(a_last, (SSD_STATE, LANES)))), axis=1)

    bc = conv_silu(SSD_INNER, BC_WIDTH)
    lane = lax.broadcasted_iota(jnp.int32, (t, LANES), 1)
    first_head = lane < SSD_HEAD_DIM
    ek = lax.broadcasted_iota(jnp.int32, (2 * LANES, GROUP_WIDTH), 0) % LANES
    ec = lax.broadcasted_iota(jnp.int32, (2 * LANES, GROUP_WIDTH), 1)
    for g in range(SSD_GROUPS):
        gs = pl.ds(g * GROUP_WIDTH, GROUP_WIDTH)
        expand2 = jnp.where(ec // SSD_HEAD_DIM + g * HEADS_PER_GROUP == ek, 1.0, 0.0).astype(BF16)
        expand = lambda parts: jnp.dot(parts, expand2, preferred_element_type=F32)

        xs = conv_silu(g * GROUP_WIDTH, GROUP_WIDTH)
        x_dt = xs * expand(dt_parts)
        x_dt_b = x_dt.astype(BF16)
        x_ds_b = (x_dt * jnp.exp(expand(tail_parts))).astype(BF16)
        y_ref[:, gs] = xs * dsk_ref[:, gs]

        b_f = bc[:, g * SSD_STATE:(g + 1) * SSD_STATE]
        b_g = b_f.astype(BF16)
        b_t = b_f.T.astype(BF16)
        c_g = bc[:, (SSD_GROUPS + g) * SSD_STATE:(SSD_GROUPS + g + 1) * SSD_STATE].astype(BF16)
        scores = lax.dot_general(c_g, b_g, (((1,), (1,)), ((), ())), preferred_element_type=F32)
        for pr in range(HEADS_PER_GROUP // 2):
            x_pair = x_dt_b[:, pr * LANES:(pr + 1) * LANES]
            halves = []
            for hh in range(2):
                h = g * HEADS_PER_GROUP + pr * 2 + hh
                seg = a_cs[:, h:h + 1] - a_cs_t[h:h + 1, :]
                decay = jnp.exp(jnp.where(causal, seg, -jnp.inf))
                attn = (scores * decay).astype(BF16)
                halves.append(jnp.dot(attn, x_pair, preferred_element_type=F32))
            cols = pl.ds(g * GROUP_WIDTH + pr * LANES, LANES)
            y_ref[:, cols] = y_ref[:, cols] + jnp.where(first_head, halves[0], halves[1])

        st = state_ref[g]
        y_off = jnp.dot(c_g, st.astype(BF16), preferred_element_type=F32)
        y_ref[:, gs] = y_ref[:, gs] + y_off * jnp.exp(expand(acs_parts))
        upd = jnp.dot(b_t, x_ds_b, preferred_element_type=F32)
        state_ref[g] = st * expand(decay_parts) + upd

    z = jnp.concatenate([ref[rows, :] for ref in z_refs], axis=1).astype(F32)
    gated = y_ref[...] * _silu(z)
    for g in range(SSD_GROUPS):
        gg = gated[:, g * GROUP_WIDTH:(g + 1) * GROUP_WIDTH]
        ms = jnp.mean(gg * gg, axis=-1, keepdims=True)
        yn_ref[rows, pl.ds(g * GROUP_WIDTH, GROUP_WIDTH)] = (
            gg * lax.rsqrt(ms + EPS) * nw_ref[:, pl.ds(g * GROUP_WIDTH, GROUP_WIDTH)]).astype(BF16)


def _ssd(proj, dt_raw, conv_w, conv_b, dt_bias, a_log, d_skip_e, ssd_norm, bsz, seq, t, nsub):
    m = bsz * seq
    rows = t * nsub
    nc = seq // rows
    row = lambda b, c: b * nc + c
    const = lambda b, c: (0, 0)
    chunk = lambda col: pl.BlockSpec(
        (None, rows, PROJ_CHUNK), lambda b, c: (col // PROJ_CHUNK, row(b, c), 0))
    return pl.pallas_call(
        functools.partial(_ssd_kernel, t=t, nsub=nsub),
        grid=(bsz, nc),
        in_specs=[
            chunk(COL_ZSSD), chunk(COL_ZSSD + PROJ_CHUNK),
            chunk(COL_XS), chunk(COL_XS + PROJ_CHUNK), chunk(COL_BC),
            pl.BlockSpec((rows, LANES), lambda b, c: (row(b, c), 0)),
            pl.BlockSpec((SSD_CONV, CONV_DIM), const),
            pl.BlockSpec((1, CONV_DIM), const),
            pl.BlockSpec((1, LANES), const),
            pl.BlockSpec((1, LANES), const),
            pl.BlockSpec((1, SSD_INNER), const),
            pl.BlockSpec((1, SSD_INNER), const),
        ],
        out_specs=pl.BlockSpec((rows, SSD_INNER), lambda b, c: (row(b, c), 0)),
        out_shape=jax.ShapeDtypeStruct((m, SSD_INNER), BF16),
        scratch_shapes=[
            pltpu.VMEM((SSD_GROUPS, SSD_STATE, GROUP_WIDTH), F32),
            pltpu.VMEM((rows + CONV_HALO, CONV_DIM), BF16),
            pltpu.VMEM((nsub, t, SSD_INNER), F32),
        ],
        compiler_params=pltpu.CompilerParams(
            dimension_semantics=("parallel", "arbitrary"), vmem_limit_bytes=VMEM_LIMIT),
        name="ssd",
    )(proj, proj, proj, proj, proj, dt_raw, conv_w, conv_b, dt_bias, a_log, d_skip_e, ssd_norm)


def _suffix_sum_matrix():
    j = np.arange(2 * SB_BLOCK)[:, None]
    n = np.arange(2 * SB_BLOCK)[None, :]
    return (j > n).astype(np.float32)


def _sb_kernel(q_ref, k_ref, v_ref, zsb_ref, uu_ref, o_ref, qm_ref, acc_ref, car_ref, gate_ref,
               flag_ref, *, qt, ng):
    tile = pl.program_id(2)
    blk = SB_BLOCK
    gw = SB_GROUP * SB_HEAD_DIM
    flag_ref[0] = jnp.int32(0)
    uu = uu_ref[...]

    def prepare():
        q_head = lax.broadcasted_iota(jnp.int32, (qt, gw), 1) // SB_HEAD_DIM
        for g in range(ng):
            q = q_ref[:, pl.ds(g * gw, gw)] * jnp.asarray(SB_HEAD_DIM ** -0.5, BF16)
            for h in range(SB_GROUP):
                qm_ref[g * SB_GROUP + h] = jnp.where(q_head == h, q, jnp.zeros_like(q))
        acc_ref[...] = jnp.zeros_like(acc_ref)
        car_ref[...] = jnp.zeros_like(car_ref)
        gate_ref[...] = _silu(zsb_ref[...].astype(F32))

    def keep_mask(spec):
        j0, _, r0, nrows, diag_block = spec
        row = lax.broadcasted_iota(jnp.int32, (SB_GROUP * nrows, blk), 0) % nrows
        qi = tile * qt + r0 + row
        ki = (j0 + diag_block) * blk + lax.broadcasted_iota(jnp.int32, (SB_GROUP * nrows, blk), 1)
        return ki < qi

    def mask_diagonal(spec, x):
        nblk, diag_block = spec[1], spec[4]
        if diag_block is None:
            return x
        parts = [x[:, c * blk:(c + 1) * blk] for c in range(nblk)]
        parts[diag_block] = jnp.where(keep_mask(spec), parts[diag_block], 0.0)
        return jnp.concatenate(parts, axis=1) if nblk > 1 else parts[0]

    def key_rows(spec):
        j0, nblk = spec[0], spec[1]
        return pl.ds(pl.multiple_of(j0 * blk, blk), nblk * blk)

    def logits_stage(spec, g):
        rows = pl.ds(spec[2], spec[3])
        q4 = jnp.concatenate([qm_ref[g * SB_GROUP + h, rows, :] for h in range(SB_GROUP)], axis=0)
        k2 = k_ref[key_rows(spec), pl.ds(g * gw, gw)]
        return lax.dot_general(q4, k2, (((1,), (1,)), ((), ())),
                               preferred_element_type=F32)

    def sums_stage(spec, s):
        span = spec[1] * blk
        sp = jnp.where(s > SB_SOFTPLUS_LINEAR, s, jnp.log(1.0 + jnp.exp(s)))
        terms = mask_diagonal(spec, sp).astype(BF16)
        suffix = jnp.dot(terms, uu[:span, :span], preferred_element_type=F32)
        return s - sp, suffix, suffix[:, :1] + terms[:, :1].astype(F32)

    def output_stage(spec, g, log_beta, suffix, total):
        nrows = spec[3]
        rows = pl.ds(spec[2], nrows)
        car = jnp.concatenate([car_ref[g * SB_GROUP + h, rows, :] for h in range(SB_GROUP)], axis=0)
        new_car = car + total
        w = mask_diagonal(spec, jnp.exp(log_beta - suffix - car)).astype(BF16)
        for h in range(SB_GROUP):
            car_ref[g * SB_GROUP + h, rows, :] = new_car[h * nrows:(h + 1) * nrows]
        v2 = v_ref[key_rows(spec), pl.ds(g * gw, gw)]
        pv4 = jnp.dot(w, v2, preferred_element_type=F32)
        lane_head = lax.broadcasted_iota(jnp.int32, (nrows, gw), 1) // SB_HEAD_DIM
        pv = pv4[:nrows]
        for h in range(1, SB_GROUP):
            pv = jnp.where(lane_head == h, pv4[h * nrows:(h + 1) * nrows], pv)
        acc_ref[rows, pl.ds(g * gw, gw)] = acc_ref[rows, pl.ds(g * gw, gw)] + pv

    def run(specs):
        items = [(spec, g) for spec in specs for g in range(ng)]
        logits, mids = {}, {}
        for i in range(len(items) + 2 * SB_LEAD):
            if i < len(items):
                logits[i] = logits_stage(*items[i])
            j = i - SB_LEAD
            if 0 <= j < len(items):
                mids[j] = sums_stage(items[j][0], logits.pop(j))
            j = i - 2 * SB_LEAD
            if 0 <= j < len(items):
                output_stage(*items[j], *mids.pop(j))

    nsub = qt // blk
    diagonal = []
    for c in reversed(range(0, nsub, 2)):
        diagonal.append((tile * nsub + c, 2, (c + 1) * blk, qt - (c + 1) * blk, 1))
        diagonal.append((tile * nsub + c, 1, c * blk, blk, 0))

    def left_spec(it):
        return (tile * nsub - 2 * (it + 1), 2, 0, qt, None)

    def flag_saturation():
        flag_ref[0] = (jnp.min(car_ref[...]) >= SB_SATURATED).astype(jnp.int32)

    @pl.when(tile == 0)
    def _():
        prepare()
        run(diagonal)

    assert nsub == 2
    first = tile * nsub
    quarter = qt // 4

    @pl.when(tile > 0)
    def _():
        prepare()
        run(diagonal + [(first - 2, 2, 0, quarter, None),
                        (first - 1, 1, quarter, 2 * quarter, None)])
        flag_saturation()

    @pl.when(jnp.logical_and(tile > 0, flag_ref[0] == 0))
    def _():
        run([(first - 1, 1, 3 * quarter, quarter, None),
             (first - 2, 1, quarter, 3 * quarter, None)])
        flag_saturation()

    n_steps = tile * (nsub // 2)

    def more(it):
        return jnp.logical_and(it < n_steps, flag_ref[0] == 0)

    def body(it):
        run([left_spec(it)])
        flag_saturation()
        return it + 1

    lax.while_loop(more, body, jnp.int32(1))
    o_ref[...] = (acc_ref[...] * gate_ref[...]).astype(BF16)


def _stick_breaking(proj, uu, bsz, seq, qt, ng):
    m = bsz * seq
    nq = seq // qt
    width = ng * SB_GROUP * SB_HEAD_DIM
    nh = ng * SB_GROUP
    assert width == PROJ_CHUNK
    return pl.pallas_call(
        functools.partial(_sb_kernel, qt=qt, ng=ng),
        grid=(bsz, SB_HEADS // nh, nq),
        in_specs=[
            pl.BlockSpec((None, qt, width), lambda b, p, i: (COL_Q // width + p, b * nq + i, 0)),
            pl.BlockSpec((None, seq, width), lambda b, p, i: (COL_K // width + p, b, 0)),
            pl.BlockSpec((None, seq, width), lambda b, p, i: (COL_V // width + p, b, 0)),
            pl.BlockSpec((None, qt, width), lambda b, p, i: (COL_ZSB // width + p, b * nq + i, 0)),
            pl.BlockSpec((2 * SB_BLOCK, 2 * SB_BLOCK), lambda b, p, i: (0, 0)),
        ],
        out_specs=pl.BlockSpec((qt, width), lambda b, p, i: (b * nq + i, p)),
        out_shape=jax.ShapeDtypeStruct((m, SB_INNER), BF16),
        scratch_shapes=[
            pltpu.VMEM((nh, qt, SB_GROUP * SB_HEAD_DIM), BF16),
            pltpu.VMEM((qt, width), F32),
            pltpu.VMEM((nh, qt, 1), F32),
            pltpu.VMEM((qt, width), F32),
            pltpu.SMEM((1,), jnp.int32),
        ],
        compiler_params=pltpu.CompilerParams(
            dimension_semantics=("parallel", "parallel", "arbitrary"),
            vmem_limit_bytes=VMEM_LIMIT),
        name="stick_breaking",
    )(proj, proj, proj, proj, uu)


def _merge_kernel(yn_ref, osb_ref, g0_ref, g1_ref, g2_ref, g3_ref, x_ref, bg_ref, wssd_ref,
                  wsb_ref, wout_ref, nw_ref, out_ref, w_ref):
    @pl.when(pl.program_id(0) == 0)
    def _():
        for k, ref in enumerate((wssd_ref, wsb_ref, wout_ref)):
            w_ref[k] = ref[...].astype(BF16)

    y_ssd = jnp.dot(yn_ref[...], w_ref[0], preferred_element_type=F32)
    y_sb = jnp.dot(osb_ref[...], w_ref[1], preferred_element_type=F32)
    gates = jnp.concatenate([g0_ref[...], g1_ref[...], g2_ref[...], g3_ref[...]], axis=1)
    g = _sigmoid(gates.astype(F32) + bg_ref[...])
    merged = (g[:, :D_MODEL] * y_ssd + g[:, D_MODEL:] * y_sb).astype(BF16)
    out = jnp.dot(merged, w_ref[2], preferred_element_type=F32)
    ms = jnp.mean(out * out, axis=-1, keepdims=True)
    out_ref[...] = x_ref[...] + out * lax.rsqrt(ms + EPS) * nw_ref[...]


def _merge(yn, osb, proj, x2, b_gate, w_ssd, w_sb, w_out, norm_post, tm):
    m = x2.shape[0]
    rows = lambda i: (i, 0)
    const = lambda i: (0, 0)
    gate = lambda k: pl.BlockSpec(
        (None, tm, PROJ_CHUNK), lambda i: (COL_GATE // PROJ_CHUNK + k, i, 0))
    weight = pl.BlockSpec((D_MODEL, D_MODEL), const, pipeline_mode=pl.Buffered(1))
    return pl.pallas_call(
        _merge_kernel,
        grid=(m // tm,),
        in_specs=[
            pl.BlockSpec((tm, D_MODEL), rows),
            pl.BlockSpec((tm, D_MODEL), rows),
            gate(0), gate(1), gate(2), gate(3),
            pl.BlockSpec((tm, D_MODEL), rows),
            pl.BlockSpec((1, 2 * D_MODEL), const),
            weight, weight, weight,
            pl.BlockSpec((1, D_MODEL), const),
        ],
        out_specs=pl.BlockSpec((tm, D_MODEL), rows),
        out_shape=jax.ShapeDtypeStruct((m, D_MODEL), F32),
        scratch_shapes=[pltpu.VMEM((3, D_MODEL, D_MODEL), BF16)],
        compiler_params=pltpu.CompilerParams(
            dimension_semantics=("arbitrary",), vmem_limit_bytes=VMEM_LIMIT),
        name="merge",
    )(yn, osb, proj, proj, proj, proj, x2, b_gate, w_ssd, w_sb, w_out, norm_post)


def _layer(x, norm_pre, w_in, b_gate, conv_w, conv_b, dt_bias, a_log, d_skip, ssd_norm,
           w_ssd_proj, w_sb_proj, w_out, norm_post):
    bsz, seq, _ = x.shape
    m = bsz * seq
    x2 = x.reshape(m, D_MODEL)

    w_r = jnp.concatenate([w_in[:, :_SRC_DT], w_in[:, _SRC_DT + SSD_HEADS:]], axis=1).astype(BF16)
    w_dt = jnp.pad(w_in[:, _SRC_DT:_SRC_DT + SSD_HEADS],
                   ((0, 0), (0, LANES - SSD_HEADS))).astype(BF16)
    pad_heads = lambda v: jnp.pad(v, (0, LANES - SSD_HEADS)).reshape(1, LANES)

    tm = min(MERGE_TM, m)
    proj, dt_raw = _inproj(x2, norm_pre.reshape(1, D_MODEL), w_r, w_dt, min(INPROJ_TM, m))

    yn = _ssd(proj, dt_raw, conv_w, conv_b.reshape(1, CONV_DIM), pad_heads(dt_bias),
              pad_heads(a_log), jnp.repeat(d_skip, SSD_HEAD_DIM).reshape(1, SSD_INNER),
              ssd_norm.reshape(1, SSD_INNER), bsz, seq, SSD_CHUNK,
              min(SSD_SUBCHUNKS, seq // SSD_CHUNK))

    osb = _stick_breaking(proj, jnp.asarray(_suffix_sum_matrix(), BF16), bsz, seq,
                          min(SB_QT, seq), SB_NG)

    out = _merge(yn, osb, proj, x2, b_gate.reshape(1, 2 * D_MODEL), w_ssd_proj, w_sb_proj, w_out,
                 norm_post.reshape(1, D_MODEL), tm)
    return out.reshape(bsz, seq, D_MODEL)


def kernel(x, norm_pre, w_in, b_gate, conv_w, conv_b, dt_bias, a_log, d_skip, ssd_norm,
           w_ssd_proj, w_sb_proj, w_out, norm_post):
    for layer in range(norm_pre.shape[0]):
        x = _layer(x, norm_pre[layer], w_in[layer], b_gate[layer], conv_w[layer],
                   conv_b[layer], dt_bias[layer], a_log[layer], d_skip[layer],
                   ssd_norm[layer], w_ssd_proj[layer], w_sb_proj[layer], w_out[layer],
                   norm_post[layer])
    return x
```

```python
import functools

import numpy as np
import jax
import jax.numpy as jnp
from jax import lax
from jax.experimental import pallas as pl
from jax.experimental.pallas import tpu as pltpu

F32 = jnp.float32
BF16 = jnp.bfloat16

D_MODEL = 1024
EPS = 1e-6
LANES = 128

SSD_HEADS = 16
SSD_HEAD_DIM = 64
SSD_INNER = SSD_HEADS * SSD_HEAD_DIM
SSD_GROUPS = 2
SSD_STATE = 128
SSD_CONV = 4
HEADS_PER_GROUP = SSD_HEADS // SSD_GROUPS
GROUP_WIDTH = SSD_INNER // SSD_GROUPS
BC_WIDTH = 2 * SSD_GROUPS * SSD_STATE
CONV_DIM = SSD_INNER + BC_WIDTH
SSD_CHUNK = 128
SSD_SUBCHUNKS = 8
CONV_HALO = 16

SB_HEADS = 16
SB_HEAD_DIM = 64
SB_INNER = SB_HEADS * SB_HEAD_DIM
SB_BLOCK = 128
SB_QT = 256
SB_GROUP = 4
SB_NG = 2
SB_LEAD = 1
SB_SATURATED = 105.0
SB_SOFTPLUS_LINEAR = 40.0

COL_ZSSD = 0
COL_XS = COL_ZSSD + SSD_INNER
COL_BC = COL_XS + SSD_INNER
COL_Q = COL_BC + BC_WIDTH
COL_K = COL_Q + SB_INNER
COL_V = COL_K + SB_INNER
COL_ZSB = COL_V + SB_INNER
COL_GATE = COL_ZSB + SB_INNER
PROJ_COLS = COL_GATE + 2 * D_MODEL
_SRC_DT = COL_Q
PROJ_CHUNK = 512

VMEM_LIMIT = 56 * 1024 * 1024
INPROJ_TM = 2048
MERGE_TM = 512


def _softplus(x):
    return jnp.maximum(x, 0.0) + jnp.log1p(jnp.exp(-jnp.abs(x)))


def _sigmoid(x):
    return 1.0 / (1.0 + jnp.exp(-x))


def _silu(x):
    half = 0.5 * x
    return half + half * jnp.tanh(half)


def _inproj_kernel(x_ref, nw_ref, w_ref, wdt_ref, proj_ref, dt_ref, h_ref):
    @pl.when(pl.program_id(1) == 0)
    def _():
        x = x_ref[...]
        ms = jnp.mean(x * x, axis=-1, keepdims=True)
        h = (x * lax.rsqrt(ms + EPS) * nw_ref[...]).astype(BF16)
        h_ref[...] = h
        dt_ref[...] = jnp.dot(h, wdt_ref[...], preferred_element_type=F32)

    tn = PROJ_CHUNK
    w = w_ref[:, pl.ds(pl.multiple_of(pl.program_id(1) * tn, tn), tn)]
    proj_ref[...] = jnp.dot(h_ref[...], w, preferred_element_type=F32).astype(BF16)


def _inproj(x2, norm_pre, w_r, w_dt, tm):
    m = x2.shape[0]
    tn = PROJ_CHUNK
    return pl.pallas_call(
        _inproj_kernel,
        grid=(m // tm, PROJ_COLS // tn),
        in_specs=[
            pl.BlockSpec((tm, D_MODEL), lambda i, j: (i, 0)),
            pl.BlockSpec((1, D_MODEL), lambda i, j: (0, 0)),
            pl.BlockSpec((D_MODEL, PROJ_COLS), lambda i, j: (0, 0), pipeline_mode=pl.Buffered(1)),
            pl.BlockSpec((D_MODEL, LANES), lambda i, j: (0, 0), pipeline_mode=pl.Buffered(1)),
        ],
        out_specs=[
            pl.BlockSpec((None, tm, tn), lambda i, j: (j, i, 0)),
            pl.BlockSpec((tm, LANES), lambda i, j: (i, 0)),
        ],
        out_shape=[
            jax.ShapeDtypeStruct((PROJ_COLS // tn, m, tn), BF16),
            jax.ShapeDtypeStruct((m, LANES), F32),
        ],
        scratch_shapes=[pltpu.VMEM((tm, D_MODEL), BF16)],
        compiler_params=pltpu.CompilerParams(
            dimension_semantics=("parallel", "arbitrary"), vmem_limit_bytes=VMEM_LIMIT),
        name="inproj",
    )(x2, norm_pre, w_r, w_dt)


def _split2(x):
    hi = x.astype(BF16)
    lo = (x - hi.astype(F32)).astype(BF16)
    return hi, lo


def _ssd_kernel(z0_ref, z1_ref, xs0_ref, xs1_ref, bc_ref, dt_ref, cw_ref, cb_ref, dtb_ref,
                alog_ref, dsk_ref, nw_ref, yn_ref, state_ref, xe_ref, y_ref, *, t, nsub):
    c = pl.program_id(1)

    @pl.when(c == 0)
    def _():
        state_ref[...] = jnp.zeros_like(state_ref)
        xe_ref[pl.ds(nsub * t, CONV_HALO), :] = jnp.zeros((CONV_HALO, CONV_DIM), BF16)

    xe_ref[pl.ds(0, CONV_HALO), :] = xe_ref[pl.ds(nsub * t, CONV_HALO), :]
    for k, ref in enumerate((xs0_ref, xs1_ref, bc_ref)):
        xe_ref[pl.ds(CONV_HALO, nsub * t), pl.ds(k * PROJ_CHUNK, PROJ_CHUNK)] = ref[...]
    for sub in range(nsub):
        _ssd_subchunk(sub, t, (z0_ref, z1_ref), dt_ref, cw_ref, cb_ref, dtb_ref, alog_ref,
                      dsk_ref, nw_ref, yn_ref, state_ref, xe_ref, y_ref)


def _ssd_subchunk(sub, t, z_refs, dt_ref, cw_ref, cb_ref, dtb_ref, alog_ref, dsk_ref, nw_ref,
                  yn_ref, state_ref, xe_ref, y_ref):
    rows = pl.ds(sub * t, t)
    y_ref = y_ref.at[sub]
    out_row = lax.broadcasted_iota(jnp.int32, (t, t + CONV_HALO), 0)
    in_row = lax.broadcasted_iota(jnp.int32, (t, t + CONV_HALO), 1)
    shifts = jnp.concatenate(
        [jnp.where(in_row == out_row + (CONV_HALO - (SSD_CONV - 1 - j)), 1.0, 0.0).astype(BF16)
         for j in range(SSD_CONV - 1)], axis=0)

    def conv_silu(c0, width):
        cols = pl.ds(c0, width)
        xe = xe_ref[pl.ds(sub * t, t + CONV_HALO), cols]
        conv = cb_ref[:, cols] + cw_ref[pl.ds(SSD_CONV - 1, 1), cols] * xe[CONV_HALO:, :].astype(F32)
        shifted = jnp.dot(shifts, xe, preferred_element_type=F32)
        for j in range(SSD_CONV - 1):
            conv = conv + cw_ref[pl.ds(j, 1), cols] * shifted[j * t:(j + 1) * t]
        return _silu(conv)

    dt = _softplus(dt_ref[rows, :] + dtb_ref[...])
    a_dt = dt * (-jnp.exp(alog_ref[...]))
    ri = lax.broadcasted_iota(jnp.int32, (t, t), 0)
    ci = lax.broadcasted_iota(jnp.int32, (t, t), 1)
    causal = ci <= ri
    tri = jnp.where(causal, 1.0, 0.0).astype(BF16)
    hi, lo = _split2(a_dt)
    lo2 = (a_dt - hi.astype(F32) - lo.astype(F32)).astype(BF16)
    a_cs = jnp.dot(jnp.concatenate([tri, tri, tri], axis=1),
                   jnp.concatenate([hi, lo, lo2], axis=0), preferred_element_type=F32)
    a_cs_t = a_cs.T
    a_last = a_cs[t - 1:t, :]
    dt_parts = jnp.concatenate(_split2(dt), axis=1)
    acs_parts = jnp.concatenate(_split2(a_cs), axis=1)
    tail_parts = jnp.concatenate(_split2(a_last - a_cs), axis=1)
    decay_parts = jnp.concatenate(
        _split2(jnp.exp(jnp.broadcast_to(a_last, (SSD_STATE, LANES)))), axis=1)

    bc = conv_silu(SSD_INNER, BC_WIDTH)
    lane = lax.broadcasted_iota(jnp.int32, (t, LANES), 1)
    first_head = lane < SSD_HEAD_DIM
    ek = lax.broadcasted_iota(jnp.int32, (2 * LANES, GROUP_WIDTH), 0) % LANES
    ec = lax.broadcasted_iota(jnp.int32, (2 * LANES, GROUP_WIDTH), 1)
    for g in range(SSD_GROUPS):
        gs = pl.ds(g * GROUP_WIDTH, GROUP_WIDTH)
        expand2 = jnp.where(ec // SSD_HEAD_DIM + g * HEADS_PER_GROUP == ek, 1.0, 0.0).astype(BF16)
        expanded = jnp.dot(jnp.concatenate([dt_parts, tail_parts, acs_parts, decay_parts], axis=0),
                           expand2, preferred_element_type=F32)
        dt_e, tail_e, acs_e, decay_e = (expanded[:t], expanded[t:2 * t], expanded[2 * t:3 * t],
                                        expanded[3 * t:])

        xs = conv_silu(g * GROUP_WIDTH, GROUP_WIDTH)
        x_dt = xs * dt_e
        x_dt_b = x_dt.astype(BF16)
        x_ds_b = (x_dt * jnp.exp(tail_e)).astype(BF16)
        y_ref[:, gs] = xs * dsk_ref[:, gs]

        b_f = bc[:, g * SSD_STATE:(g + 1) * SSD_STATE]
        b_g = b_f.astype(BF16)
        b_t = b_f.T.astype(BF16)
        c_g = bc[:, (SSD_GROUPS + g) * SSD_STATE:(SSD_GROUPS + g + 1) * SSD_STATE].astype(BF16)
        scores = lax.dot_general(c_g, b_g, (((1,), (1,)), ((), ())), preferred_element_type=F32)
        for pr in range(HEADS_PER_GROUP // 2):
            x_pair = x_dt_b[:, pr * LANES:(pr + 1) * LANES]
            attn = []
            for hh in range(2):
                h = g * HEADS_PER_GROUP + pr * 2 + hh
                seg = a_cs[:, h:h + 1] - a_cs_t[h:h + 1, :]
                decay = jnp.exp(jnp.where(causal, seg, -jnp.inf))
                attn.append((scores * decay).astype(BF16))
            both = jnp.dot(jnp.concatenate(attn, axis=0), x_pair, preferred_element_type=F32)
            cols = pl.ds(g * GROUP_WIDTH + pr * LANES, LANES)
            y_ref[:, cols] = y_ref[:, cols] + jnp.where(first_head, both[:t], both[t:])

        st = state_ref[g]
        y_off = jnp.dot(c_g, st.astype(BF16), preferred_element_type=F32)
        y_ref[:, gs] = y_ref[:, gs] + y_off * jnp.exp(acs_e)
        upd = jnp.dot(b_t, x_ds_b, preferred_element_type=F32)
        state_ref[g] = st * decay_e + upd

    z = jnp.concatenate([ref[rows, :] for ref in z_refs], axis=1).astype(F32)
    gated = y_ref[...] * _silu(z)
    for g in range(SSD_GROUPS):
        gg = gated[:, g * GROUP_WIDTH:(g + 1) * GROUP_WIDTH]
        ms = jnp.mean(gg * gg, axis=-1, keepdims=True)
        yn_ref[rows, pl.ds(g * GROUP_WIDTH, GROUP_WIDTH)] = (
            gg * lax.rsqrt(ms + EPS) * nw_ref[:, pl.ds(g * GROUP_WIDTH, GROUP_WIDTH)]).astype(BF16)


def _ssd(proj, dt_raw, conv_w, conv_b, dt_bias, a_log, d_skip_e, ssd_norm, bsz, seq, t, nsub):
    m = bsz * seq
    rows = t * nsub
    nc = seq // rows
    row = lambda b, c: b * nc + c
    const = lambda b, c: (0, 0)
    chunk = lambda col: pl.BlockSpec(
        (None, rows, PROJ_CHUNK), lambda b, c: (col // PROJ_CHUNK, row(b, c), 0))
    return pl.pallas_call(
        functools.partial(_ssd_kernel, t=t, nsub=nsub),
        grid=(bsz, nc),
        in_specs=[
            chunk(COL_ZSSD), chunk(COL_ZSSD + PROJ_CHUNK),
            chunk(COL_XS), chunk(COL_XS + PROJ_CHUNK), chunk(COL_BC),
            pl.BlockSpec((rows, LANES), lambda b, c: (row(b, c), 0)),
            pl.BlockSpec((SSD_CONV, CONV_DIM), const),
            pl.BlockSpec((1, CONV_DIM), const),
            pl.BlockSpec((1, LANES), const),
            pl.BlockSpec((1, LANES), const),
            pl.BlockSpec((1, SSD_INNER), const),
            pl.BlockSpec((1, SSD_INNER), const),
        ],
        out_specs=pl.BlockSpec((rows, SSD_INNER), lambda b, c: (row(b, c), 0)),
        out_shape=jax.ShapeDtypeStruct((m, SSD_INNER), BF16),
        scratch_shapes=[
            pltpu.VMEM((SSD_GROUPS, SSD_STATE, GROUP_WIDTH), F32),
            pltpu.VMEM((rows + CONV_HALO, CONV_DIM), BF16),
            pltpu.VMEM((nsub, t, SSD_INNER), F32),
        ],
        compiler_params=pltpu.CompilerParams(
            dimension_semantics=("parallel", "arbitrary"), vmem_limit_bytes=VMEM_LIMIT),
        name="ssd",
    )(proj, proj, proj, proj, proj, dt_raw, conv_w, conv_b, dt_bias, a_log, d_skip_e, ssd_norm)


def _suffix_sum_matrix():
    j = np.arange(2 * SB_BLOCK)[:, None]
    n = np.arange(2 * SB_BLOCK)[None, :]
    return (j > n).astype(np.float32)


def _sb_kernel(q_ref, k_ref, v_ref, zsb_ref, uu_ref, o_ref, qm_ref, acc_ref, car_ref, gate_ref,
               flag_ref, *, qt, ng):
    tile = pl.program_id(2)
    blk = SB_BLOCK
    gw = SB_GROUP * SB_HEAD_DIM
    flag_ref[0] = jnp.int32(0)
    uu = uu_ref[...]

    def prepare():
        q_head = lax.broadcasted_iota(jnp.int32, (qt, gw), 1) // SB_HEAD_DIM
        for g in range(ng):
            q = q_ref[:, pl.ds(g * gw, gw)] * jnp.asarray(SB_HEAD_DIM ** -0.5, BF16)
            for h in range(SB_GROUP):
                qm_ref[g * SB_GROUP + h] = jnp.where(q_head == h, q, jnp.zeros_like(q))
        acc_ref[...] = jnp.zeros_like(acc_ref)
        car_ref[...] = jnp.zeros_like(car_ref)
        gate_ref[...] = _silu(zsb_ref[...].astype(F32))

    def keep_mask(spec):
        j0, _, r0, nrows, diag_block = spec
        row = lax.broadcasted_iota(jnp.int32, (SB_GROUP * nrows, blk), 0) % nrows
        qi = tile * qt + r0 + row
        ki = (j0 + diag_block) * blk + lax.broadcasted_iota(jnp.int32, (SB_GROUP * nrows, blk), 1)
        return ki < qi

    def mask_diagonal(spec, x):
        nblk, diag_block = spec[1], spec[4]
        if diag_block is None:
            return x
        parts = [x[:, c * blk:(c + 1) * blk] for c in range(nblk)]
        parts[diag_block] = jnp.where(keep_mask(spec), parts[diag_block], 0.0)
        return jnp.concatenate(parts, axis=1) if nblk > 1 else parts[0]

    def key_rows(spec):
        j0, nblk = spec[0], spec[1]
        return pl.ds(pl.multiple_of(j0 * blk, blk), nblk * blk)

    def logits_stage(spec, g):
        rows = pl.ds(spec[2], spec[3])
        q4 = jnp.concatenate([qm_ref[g * SB_GROUP + h, rows, :] for h in range(SB_GROUP)], axis=0)
        k2 = k_ref[key_rows(spec), pl.ds(g * gw, gw)]
        return lax.dot_general(q4, k2, (((1,), (1,)), ((), ())),
                               preferred_element_type=F32)

    def sums_stage(spec, s):
        span = spec[1] * blk
        sp = jnp.where(s > SB_SOFTPLUS_LINEAR, s, jnp.log(1.0 + jnp.exp(s)))
        terms = mask_diagonal(spec, sp).astype(BF16)
        suffix = jnp.dot(terms, uu[:span, :span], preferred_element_type=F32)
        return s - sp, suffix, suffix[:, :1] + terms[:, :1].astype(F32)

    def output_stage(spec, g, log_beta, suffix, total):
        nrows = spec[3]
        rows = pl.ds(spec[2], nrows)
        car = jnp.concatenate([car_ref[g * SB_GROUP + h, rows, :] for h in range(SB_GROUP)], axis=0)
        new_car = car + total
        w = mask_diagonal(spec, jnp.exp(log_beta - suffix - car)).astype(BF16)
        for h in range(SB_GROUP):
            car_ref[g * SB_GROUP + h, rows, :] = new_car[h * nrows:(h + 1) * nrows]
        v2 = v_ref[key_rows(spec), pl.ds(g * gw, gw)]
        pv4 = jnp.dot(w, v2, preferred_element_type=F32)
        lane_head = lax.broadcasted_iota(jnp.int32, (nrows, gw), 1) // SB_HEAD_DIM
        pv = pv4[:nrows]
        for h in range(1, SB_GROUP):
            pv = jnp.where(lane_head == h, pv4[h * nrows:(h + 1) * nrows], pv)
        acc_ref[rows, pl.ds(g * gw, gw)] = acc_ref[rows, pl.ds(g * gw, gw)] + pv

    def run(specs):
        items = [(spec, g) for spec in specs for g in range(ng)]
        logits, mids = {}, {}
        for i in range(len(items) + 2 * SB_LEAD):
            if i < len(items):
                logits[i] = logits_stage(*items[i])
            j = i - SB_LEAD
            if 0 <= j < len(items):
                mids[j] = sums_stage(items[j][0], logits.pop(j))
            j = i - 2 * SB_LEAD
            if 0 <= j < len(items):
                output_stage(*items[j], *mids.pop(j))

    nsub = qt // blk
    diagonal = []
    for c in reversed(range(0, nsub, 2)):
        diagonal.append((tile * nsub + c, 2, (c + 1) * blk, qt - (c + 1) * blk, 1))
        diagonal.append((tile * nsub + c, 1, c * blk, blk, 0))

    def left_spec(it):
        return (tile * nsub - 2 * (it + 1), 2, 0, qt, None)

    def flag_saturation():
        flag_ref[0] = (jnp.min(car_ref[...]) >= SB_SATURATED).astype(jnp.int32)

    @pl.when(tile == 0)
    def _():
        prepare()
        run(diagonal)

    assert nsub == 2
    first = tile * nsub
    quarter = qt // 4

    @pl.when(tile > 0)
    def _():
        prepare()
        run(diagonal + [(first - 2, 2, 0, quarter, None),
                        (first - 1, 1, quarter, 2 * quarter, None)])
        flag_saturation()

    @pl.when(jnp.logical_and(tile > 0, flag_ref[0] == 0))
    def _():
        run([(first - 1, 1, 3 * quarter, quarter, None),
             (first - 2, 1, quarter, 3 * quarter, None)])
        flag_saturation()

    n_steps = tile * (nsub // 2)

    def more(it):
        return jnp.logical_and(it < n_steps, flag_ref[0] == 0)

    def body(it):
        run([left_spec(it)])
        flag_saturation()
        return it + 1

    lax.while_loop(more, body, jnp.int32(1))
    o_ref[...] = (acc_ref[...] * gate_ref[...]).astype(BF16)


def _stick_breaking(proj, uu, bsz, seq, qt, ng):
    m = bsz * seq
    nq = seq // qt
    width = ng * SB_GROUP * SB_HEAD_DIM
    nh = ng * SB_GROUP
    assert width == PROJ_CHUNK
    return pl.pallas_call(
        functools.partial(_sb_kernel, qt=qt, ng=ng),
        grid=(bsz, SB_HEADS // nh, nq),
        in_specs=[
            pl.BlockSpec((None, qt, width), lambda b, p, i: (COL_Q // width + p, b * nq + i, 0)),
            pl.BlockSpec((None, seq, width), lambda b, p, i: (COL_K // width + p, b, 0)),
            pl.BlockSpec((None, seq, width), lambda b, p, i: (COL_V // width + p, b, 0)),
            pl.BlockSpec((None, qt, width), lambda b, p, i: (COL_ZSB // width + p, b * nq + i, 0)),
            pl.BlockSpec((2 * SB_BLOCK, 2 * SB_BLOCK), lambda b, p, i: (0, 0)),
        ],
        out_specs=pl.BlockSpec((qt, width), lambda b, p, i: (b * nq + i, p)),
        out_shape=jax.ShapeDtypeStruct((m, SB_INNER), BF16),
        scratch_shapes=[
            pltpu.VMEM((nh, qt, SB_GROUP * SB_HEAD_DIM), BF16),
            pltpu.VMEM((qt, width), F32),
            pltpu.VMEM((nh, qt, 1), F32),
            pltpu.VMEM((qt, width), F32),
            pltpu.SMEM((1,), jnp.int32),
        ],
        compiler_params=pltpu.CompilerParams(
            dimension_semantics=("parallel", "parallel", "arbitrary"),
            vmem_limit_bytes=VMEM_LIMIT),
        name="stick_breaking",
    )(proj, proj, proj, proj, uu)


def _merge_kernel(yn_ref, osb_ref, g0_ref, g1_ref, g2_ref, g3_ref, x_ref, bg_ref, wssd_ref,
                  wsb_ref, wout_ref, nw_ref, out_ref, w_ref):
    @pl.when(pl.program_id(0) == 0)
    def _():
        for k, ref in enumerate((wssd_ref, wsb_ref, wout_ref)):
            w_ref[k] = ref[...].astype(BF16)

    y_ssd = jnp.dot(yn_ref[...], w_ref[0], preferred_element_type=F32)
    y_sb = jnp.dot(osb_ref[...], w_ref[1], preferred_element_type=F32)
    gates = jnp.concatenate([g0_ref[...], g1_ref[...], g2_ref[...], g3_ref[...]], axis=1)
    g = _sigmoid(gates.astype(F32) + bg_ref[...])
    merged = (g[:, :D_MODEL] * y_ssd + g[:, D_MODEL:] * y_sb).astype(BF16)
    out = jnp.dot(merged, w_ref[2], preferred_element_type=F32)
    ms = jnp.mean(out * out, axis=-1, keepdims=True)
    out_ref[...] = x_ref[...] + out * lax.rsqrt(ms + EPS) * nw_ref[...]


def _merge(yn, osb, proj, x2, b_gate, w_ssd, w_sb, w_out, norm_post, tm):
    m = x2.shape[0]
    rows = lambda i: (i, 0)
    const = lambda i: (0, 0)
    gate = lambda k: pl.BlockSpec(
        (None, tm, PROJ_CHUNK), lambda i: (COL_GATE // PROJ_CHUNK + k, i, 0))
    weight = pl.BlockSpec((D_MODEL, D_MODEL), const, pipeline_mode=pl.Buffered(1))
    return pl.pallas_call(
        _merge_kernel,
        grid=(m // tm,),
        in_specs=[
            pl.BlockSpec((tm, D_MODEL), rows),
            pl.BlockSpec((tm, D_MODEL), rows),
            gate(0), gate(1), gate(2), gate(3),
            pl.BlockSpec((tm, D_MODEL), rows),
            pl.BlockSpec((1, 2 * D_MODEL), const),
            weight, weight, weight,
            pl.BlockSpec((1, D_MODEL), const),
        ],
        out_specs=pl.BlockSpec((tm, D_MODEL), rows),
        out_shape=jax.ShapeDtypeStruct((m, D_MODEL), F32),
        scratch_shapes=[pltpu.VMEM((3, D_MODEL, D_MODEL), BF16)],
        compiler_params=pltpu.CompilerParams(
            dimension_semantics=("arbitrary",), vmem_limit_bytes=VMEM_LIMIT),
        name="merge",
    )(yn, osb, proj, proj, proj, proj, x2, b_gate, w_ssd, w_sb, w_out, norm_post)


def _layer(x, norm_pre, w_in, b_gate, conv_w, conv_b, dt_bias, a_log, d_skip, ssd_norm,
           w_ssd_proj, w_sb_proj, w_out, norm_post):
    bsz, seq, _ = x.shape
    m = bsz * seq
    x2 = x.reshape(m, D_MODEL)

    w_r = jnp.concatenate([w_in[:, :_SRC_DT], w_in[:, _SRC_DT + SSD_HEADS:]], axis=1).astype(BF16)
    w_dt = jnp.pad(w_in[:, _SRC_DT:_SRC_DT + SSD_HEADS],
                   ((0, 0), (0, LANES - SSD_HEADS))).astype(BF16)
    pad_heads = lambda v: jnp.pad(v, (0, LANES - SSD_HEADS)).reshape(1, LANES)

    tm = min(MERGE_TM, m)
    proj, dt_raw = _inproj(x2, norm_pre.reshape(1, D_MODEL), w_r, w_dt, min(INPROJ_TM, m))

    yn = _ssd(proj, dt_raw, conv_w, conv_b.reshape(1, CONV_DIM), pad_heads(dt_bias),
              pad_heads(a_log), jnp.repeat(d_skip, SSD_HEAD_DIM).reshape(1, SSD_INNER),
              ssd_norm.reshape(1, SSD_INNER), bsz, seq, SSD_CHUNK,
              min(SSD_SUBCHUNKS, seq // SSD_CHUNK))

    osb = _stick_breaking(proj, jnp.asarray(_suffix_sum_matrix(), BF16), bsz, seq,
                          min(SB_QT, seq), SB_NG)

    out = _merge(yn, osb, proj, x2, b_gate.reshape(1, 2 * D_MODEL), w_ssd_proj, w_sb_proj, w_out,
                 norm_post.reshape(1, D_MODEL), tm)
    return out.reshape(bsz, seq, D_MODEL)


def kernel(x, norm_pre, w_in, b_gate, conv_w, conv_b, dt_bias, a_log, d_skip, ssd_norm,
           w_ssd_proj, w_sb_proj, w_out, norm_post):
    for layer in range(norm_pre.shape[0]):
        x = _layer(x, norm_pre[layer], w_in[layer], b_gate[layer], conv_w[layer],
                   conv_b[layer], dt_bias[layer], a_log[layer], d_skip[layer],
                   ssd_norm[layer], w_ssd_proj[layer], w_sb_proj[layer], w_out[layer],
                   norm_post[layer])
    return x
```
